```python
import math
import jax
import jax.numpy as jnp
from jax import lax
import numpy as np

D_MODEL = 1024
BATCH = 16
SEQ = 2048
DEPTH = 4

GRID_W = 64
CTX_LEN = 256

S5_WIDTH = D_MODEL // 2
S5_GROUP = 16
S5_GROUPS = S5_WIDTH // S5_GROUP
S5_STATE = 64
S5_DT_MIN = 0.001
S5_DT_MAX = 0.1

LRU_WIDTH = D_MODEL
LRU_BLOCKS = 16
LRU_BLOCK = LRU_WIDTH // LRU_BLOCKS
LRU_C = 8.0
CONV_W = 4

IN_WIDTH = S5_WIDTH + 2 * LRU_WIDTH + 2 * D_MODEL
N_STATE_COLS = S5_WIDTH + LRU_WIDTH

N_EXPERTS = 32
TOP_K = 4
D_EXPERT = D_MODEL
SWIGLU_LIMIT = 7.0
SWIGLU_ALPHA = 1.702

EPS = 1e-6

kernel_name = 'hybrid_s5_rglru_moe_diffusion_block'


def rmsnorm(x, g):
    xf = x.astype(jnp.float32)
    y = xf * lax.rsqrt(jnp.mean(xf * xf, axis=-1, keepdims=True) + EPS)
    return (y * g.astype(jnp.float32)).astype(x.dtype)


def modulate(h, shift, scale):
    return h * (1.0 + scale) + shift


def maybe_flip(a, rev):
    return jnp.flip(a, axis=1) if rev else a


def to_colmajor(x, rows):
    b, s, ch = x.shape
    return x.reshape(b, rows, GRID_W, ch).transpose(0, 2, 1, 3).reshape(b, s, ch)


def from_colmajor(x, rows):
    b, s, ch = x.shape
    return x.reshape(b, GRID_W, rows, ch).transpose(0, 2, 1, 3).reshape(b, s, ch)


def dwconv(x, w, bias):
    ch = x.shape[-1]
    y = lax.conv_general_dilated(x.astype(w.dtype), w[:, None, :], window_strides=(1,),
                                 padding=[(1, 2)], dimension_numbers=('NWC', 'WIO', 'NWC'),
                                 feature_group_count=ch)
    return y + bias


def real_scan(a, b, h0):
    b = b.at[:, 0].add(a[:, 0] * h0)
    def combine(e1, e2):
        return (e1[0] * e2[0], e2[0] * e1[1] + e2[1])
    return lax.associative_scan(combine, (a, b), axis=1)[1]


def complex_combine(e1, e2):
    a1r, a1i, b1r, b1i = e1
    a2r, a2i, b2r, b2i = e2
    return (a2r * a1r - a2i * a1i, a2r * a1i + a2i * a1r,
            a2r * b1r - a2i * b1i + b2r, a2r * b1i + a2i * b1r + b2i)


def s5_discretize(lam_re, lam_im, log_dt, b_re, b_im):
    f32 = jnp.float32
    lr, li = lam_re.astype(f32), lam_im.astype(f32)
    dt = jnp.exp(log_dt.astype(f32))[:, None]
    mag = jnp.exp(lr * dt)
    ab_re, ab_im = mag * jnp.cos(li * dt), mag * jnp.sin(li * dt)
    den = lr * lr + li * li
    zr = ab_re - 1.0
    q_re = (zr * lr + ab_im * li) / den
    q_im = (ab_im * lr - zr * li) / den
    br, bi = b_re.astype(f32), b_im.astype(f32)
    bb_re = q_re[..., None] * br - q_im[..., None] * bi
    bb_im = q_re[..., None] * bi + q_im[..., None] * br
    return ab_re, ab_im, bb_re, bb_im


def s5_scan(u, ab_re, ab_im, bb_re, bb_im, h0_re, h0_im):
    t = u.shape[1]
    br = jnp.einsum('btgh,gph->btgp', u, bb_re)
    bi = jnp.einsum('btgh,gph->btgp', u, bb_im)
    br = br.at[:, 0].add(ab_re * h0_re - ab_im * h0_im)
    bi = bi.at[:, 0].add(ab_re * h0_im + ab_im * h0_re)
    a_re = jnp.broadcast_to(ab_re, (1, t) + ab_re.shape)
    a_im = jnp.broadcast_to(ab_im, (1, t) + ab_im.shape)
    _, _, hr, hi = lax.associative_scan(complex_combine, (a_re, a_im, br, bi), axis=1)
    return hr, hi


def s5_readout(hr, hi, c_re, c_im):
    f32 = jnp.float32
    return (jnp.einsum('btgp,ghp->btgh', hr, c_re.astype(f32))
            - jnp.einsum('btgp,ghp->btgh', hi, c_im.astype(f32)))


def s5_post(y, u, d, w_glu, b_glu):
    y = y.reshape(u.shape[0], u.shape[1], S5_WIDTH) + d * u.reshape(u.shape[0], u.shape[1], S5_WIDTH)
    gy = jax.nn.gelu(y)
    return gy * jax.nn.sigmoid(gy @ w_glu + b_glu)


def s5_branch(u_c, u_l, need_ctx, lam_re, lam_im, log_dt, b_re, b_im, c_re, c_im, d, w_glu, b_glu):
    f32 = jnp.float32
    bsz = u_l.shape[0]
    uc = u_c.astype(f32).reshape(bsz, u_c.shape[1], S5_GROUPS, S5_GROUP)
    ul = u_l.astype(f32).reshape(bsz, u_l.shape[1], S5_GROUPS, S5_GROUP)
    h0 = jnp.zeros((bsz, S5_GROUPS, S5_STATE), f32)
    ys_l, ys_c = [], []
    for dr, rev in ((0, False), (1, True)):
        ab_re, ab_im, bb_re, bb_im = s5_discretize(lam_re[dr], lam_im[dr], log_dt[dr], b_re[dr], b_im[dr])
        hc_re, hc_im = s5_scan(maybe_flip(uc, rev), ab_re, ab_im, bb_re, bb_im, h0, h0)
        hl_re, hl_im = s5_scan(maybe_flip(ul, rev), ab_re, ab_im, bb_re, bb_im, hc_re[:, -1], hc_im[:, -1])
        ys_l.append(maybe_flip(s5_readout(hl_re, hl_im, c_re[dr], c_im[dr]), rev))
        if need_ctx:
            ys_c.append(maybe_flip(s5_readout(hc_re, hc_im, c_re[dr], c_im[dr]), rev))
    out_l = s5_post(ys_l[0] + ys_l[1], ul, d, w_glu, b_glu)
    out_c = s5_post(ys_c[0] + ys_c[1], uc, d, w_glu, b_glu) if need_ctx else None
    return out_c, out_l


def rglru_coeffs(x, lam, w_a, b_a, w_x, b_x):
    f32 = jnp.float32
    bsz, t, w = x.shape
    xb = x.reshape(bsz, t, LRU_BLOCKS, LRU_BLOCK)
    r = jax.nn.sigmoid(jnp.einsum('btnh,nhk->btnk', xb, w_a.astype(f32)).reshape(bsz, t, w) + b_a)
    i = jax.nn.sigmoid(jnp.einsum('btnh,nhk->btnk', xb, w_x.astype(f32)).reshape(bsz, t, w) + b_x)
    log_a = -LRU_C * r * jax.nn.softplus(-lam.astype(f32))
    a = jnp.exp(log_a)
    b = jnp.sqrt(-jnp.expm1(2.0 * log_a)) * (i * x)
    return a, b


def rglru_branch(x_c, x_l, g_c, g_l, rows, need_ctx, conv_w, conv_b, lam, w_a, b_a, w_x, b_x):
    f32 = jnp.float32
    xl = dwconv(to_colmajor(x_l, rows), conv_w, conv_b).astype(f32)
    xc = dwconv(x_c, conv_w, conv_b).astype(f32)
    h0 = jnp.zeros((xl.shape[0], LRU_WIDTH), f32)
    ys_l, ys_c = [], []
    for dr, rev in ((0, False), (1, True)):
        a_c, b_c = rglru_coeffs(maybe_flip(xc, rev), lam[dr], w_a[dr], b_a[dr], w_x[dr], b_x[dr])
        h_c = real_scan(a_c, b_c, h0)
        a_l, b_l = rglru_coeffs(maybe_flip(xl, rev), lam[dr], w_a[dr], b_a[dr], w_x[dr], b_x[dr])
        h_l = real_scan(a_l, b_l, h_c[:, -1])
        ys_l.append(maybe_flip(h_l, rev))
        if need_ctx:
            ys_c.append(maybe_flip(h_c, rev))
    out_l = jax.nn.gelu(g_l) * from_colmajor(ys_l[0] + ys_l[1], rows)
    out_c = jax.nn.gelu(g_c) * (ys_c[0] + ys_c[1]) if need_ctx else None
    return out_c, out_l


def mixer(h_c, h_l, rows, need_ctx, w_in, s5_params, lru_params, w_proj_s5, w_proj_lru, w_out):
    splits = [S5_WIDTH, S5_WIDTH + LRU_WIDTH, S5_WIDTH + 2 * LRU_WIDTH, S5_WIDTH + 2 * LRU_WIDTH + D_MODEL]
    u_l, xr_l, g_l, m5_l, ml_l = jnp.split(h_l @ w_in, splits, axis=-1)
    u_c, xr_c = jnp.split(h_c @ w_in[:, :N_STATE_COLS], [S5_WIDTH], axis=-1)
    if need_ctx:
        g_c, m5_c, ml_c = jnp.split(h_c @ w_in[:, N_STATE_COLS:], [LRU_WIDTH, LRU_WIDTH + D_MODEL], axis=-1)
    else:
        g_c = None
    o5_c, o5_l = s5_branch(u_c, u_l, need_ctx, *s5_params)
    ol_c, ol_l = rglru_branch(xr_c, xr_l, g_c, g_l, rows, need_ctx, *lru_params)

    def merge(o5, ol, m5, ml):
        merged = jax.nn.sigmoid(m5) * (o5 @ w_proj_s5) + jax.nn.sigmoid(ml) * (ol @ w_proj_lru)
        return merged @ w_out

    out_l = merge(o5_l, ol_l, m5_l, ml_l)
    out_c = merge(o5_c, ol_c, m5_c, ml_c) if need_ctx else None
    return out_c, out_l


def moe(h, w_router, b_router, w_up, b_up, w_down, b_down):
    f32 = jnp.float32
    logits = (h @ w_router + b_router).astype(f32)
    top_v, top_i = lax.top_k(logits, TOP_K)
    top_w = jax.nn.softmax(top_v, axis=-1)
    combine = jnp.sum(jax.nn.one_hot(top_i, N_EXPERTS, dtype=f32) * top_w[..., None], axis=1)

    def expert(acc, e):
        w1, b1, w2, b2, ce = e
        gu = h @ w1 + b1
        gate = jnp.minimum(gu[..., ::2], SWIGLU_LIMIT)
        up = jnp.clip(gu[..., 1::2], -SWIGLU_LIMIT, SWIGLU_LIMIT)
        act = gate * jax.nn.sigmoid(SWIGLU_ALPHA * gate) * (up + 1.0)
        out = act @ w2 + b2
        return acc + ce[:, None] * out.astype(f32), None

    acc0 = jnp.zeros((h.shape[0], D_MODEL), f32)
    acc, _ = lax.scan(expert, acc0, (w_up, b_up, w_down, b_down, combine.T))
    return acc


def setup_inputs(seed: int = 0) -> dict:
    key = jax.random.key(seed)
    keys = iter(jax.random.split(key, 48))
    f32 = jnp.float32

    def rnd(shape, scale):
        return scale * jax.random.normal(next(keys), shape, f32)

    L, D = DEPTH, D_MODEL
    G, P, H = S5_GROUPS, S5_STATE, S5_GROUP
    x = rnd((BATCH, SEQ, D), 1.0)
    c = rnd((BATCH, D), 1.0)
    ctx = rnd((BATCH, CTX_LEN, D), 1.0)
    c_ctx = rnd((D,), 1.0)
    w_ada = rnd((L, D, 6 * D), 0.5 * D ** -0.5)
    b_ada = rnd((L, 6 * D), 0.02)
    g_mix = 1.0 + rnd((L, D), 0.05)
    g_ffn = 1.0 + rnd((L, D), 0.05)
    w_in = rnd((L, D, IN_WIDTH), D ** -0.5)
    n_idx = jnp.arange(P, dtype=f32)
    s5_lam_re = -0.5 + rnd((L, 2, G, P), 0.01)
    s5_lam_im = math.pi * n_idx + rnd((L, 2, G, P), 0.01)
    u_dt = jax.random.uniform(next(keys), (L, 2, G), f32)
    s5_log_dt = math.log(S5_DT_MIN) + u_dt * (math.log(S5_DT_MAX) - math.log(S5_DT_MIN))
    s5_b_re = rnd((L, 2, G, P, H), (2 * H) ** -0.5)
    s5_b_im = rnd((L, 2, G, P, H), (2 * H) ** -0.5)
    s5_c_re = rnd((L, 2, G, H, P), P ** -0.5)
    s5_c_im = rnd((L, 2, G, H, P), P ** -0.5)
    s5_d = rnd((L, S5_WIDTH), 1.0)
    s5_w_glu = rnd((L, S5_WIDTH, S5_WIDTH), S5_WIDTH ** -0.5)
    s5_b_glu = rnd((L, S5_WIDTH), 0.02)
    lru_conv_w = rnd((L, CONV_W, LRU_WIDTH), CONV_W ** -0.5)
    lru_conv_b = rnd((L, LRU_WIDTH), 0.02)
    a0 = jax.random.uniform(next(keys), (L, 2, LRU_WIDTH), f32, minval=0.9, maxval=0.999)
    p = a0 ** (1.0 / LRU_C)
    lru_lam = jnp.log(p) - jnp.log1p(-p)
    lru_w_a = rnd((L, 2, LRU_BLOCKS, LRU_BLOCK, LRU_BLOCK), LRU_BLOCK ** -0.5)
    lru_b_a = rnd((L, 2, LRU_WIDTH), 0.02)
    lru_w_x = rnd((L, 2, LRU_BLOCKS, LRU_BLOCK, LRU_BLOCK), LRU_BLOCK ** -0.5)
    lru_b_x = rnd((L, 2, LRU_WIDTH), 0.02)
    w_proj_s5 = rnd((L, S5_WIDTH, D), S5_WIDTH ** -0.5)
    w_proj_lru = rnd((L, LRU_WIDTH, D), LRU_WIDTH ** -0.5)
    w_out = rnd((L, D, D), D ** -0.5)
    w_router = rnd((L, D, N_EXPERTS), D ** -0.5)
    b_router = rnd((L, N_EXPERTS), 0.01)
    w_up = rnd((L, N_EXPERTS, D, 2 * D_EXPERT), D ** -0.5)
    b_up = rnd((L, N_EXPERTS, 2 * D_EXPERT), 0.02)
    w_down = rnd((L, N_EXPERTS, D_EXPERT, D), D_EXPERT ** -0.5)
    b_down = rnd((L, N_EXPERTS, D), 0.02)
    g_final = 1.0 + rnd((D,), 0.05)
    return {'x': x, 'c': c, 'ctx': ctx, 'c_ctx': c_ctx, 'w_ada': w_ada, 'b_ada': b_ada,
            'g_mix': g_mix, 'g_ffn': g_ffn, 'w_in': w_in,
            's5_lam_re': s5_lam_re, 's5_lam_im': s5_lam_im, 's5_log_dt': s5_log_dt,
            's5_b_re': s5_b_re, 's5_b_im': s5_b_im, 's5_c_re': s5_c_re, 's5_c_im': s5_c_im,
            's5_d': s5_d, 's5_w_glu': s5_w_glu, 's5_b_glu': s5_b_glu,
            'lru_conv_w': lru_conv_w, 'lru_conv_b': lru_conv_b, 'lru_lam': lru_lam,
            'lru_w_a': lru_w_a, 'lru_b_a': lru_b_a, 'lru_w_x': lru_w_x, 'lru_b_x': lru_b_x,
            'w_proj_s5': w_proj_s5, 'w_proj_lru': w_proj_lru, 'w_out': w_out,
            'w_router': w_router, 'b_router': b_router, 'w_up': w_up, 'b_up': b_up,
            'w_down': w_down, 'b_down': b_down, 'g_final': g_final}


def reference(x, c, ctx, c_ctx, w_ada, b_ada, g_mix, g_ffn, w_in,
              s5_lam_re, s5_lam_im, s5_log_dt, s5_b_re, s5_b_im, s5_c_re, s5_c_im,
              s5_d, s5_w_glu, s5_b_glu, lru_conv_w, lru_conv_b, lru_lam,
              lru_w_a, lru_b_a, lru_w_x, lru_b_x, w_proj_s5, w_proj_lru, w_out,
              w_router, b_router, w_up, b_up, w_down, b_down, g_final):
    out_dtype = x.dtype
    seq = x.shape[1]
    rows = seq // GRID_W
    cond_l = jax.nn.silu(c)
    cond_c = jax.nn.silu(c_ctx)
    h_ctx = ctx
    for l in range(DEPTH):
        need_ctx = l < DEPTH - 1
        ada_l = cond_l @ w_ada[l] + b_ada[l]
        sh1, sc1, g1, sh2, sc2, g2 = jnp.split(ada_l[:, None, :], 6, axis=-1)
        n_c = 6 if need_ctx else 2
        ada_c = cond_c @ w_ada[l][:, :n_c * D_MODEL] + b_ada[l][:n_c * D_MODEL]
        mods_c = jnp.split(ada_c, n_c)

        hn_l = modulate(rmsnorm(x, g_mix[l]), sh1, sc1)
        hn_c = modulate(rmsnorm(h_ctx, g_mix[l]), mods_c[0], mods_c[1])
        s5_params = (s5_lam_re[l], s5_lam_im[l], s5_log_dt[l], s5_b_re[l], s5_b_im[l],
                     s5_c_re[l], s5_c_im[l], s5_d[l], s5_w_glu[l], s5_b_glu[l])
        lru_params = (lru_conv_w[l], lru_conv_b[l], lru_lam[l], lru_w_a[l], lru_b_a[l],
                      lru_w_x[l], lru_b_x[l])
        mix_c, mix_l = mixer(hn_c, hn_l, rows, need_ctx, w_in[l], s5_params, lru_params,
                             w_proj_s5[l], w_proj_lru[l], w_out[l])
        x = x + g1 * mix_l

        moe_params = (w_router[l], b_router[l], w_up[l], b_up[l], w_down[l], b_down[l])
        hn_l = modulate(rmsnorm(x, g_ffn[l]), sh2, sc2)
        if need_ctx:
            h_ctx = h_ctx + mods_c[2] * mix_c
            hn_c = modulate(rmsnorm(h_ctx, g_ffn[l]), mods_c[3], mods_c[4])
            n_ctx_tok = hn_c.shape[0] * hn_c.shape[1]
            tok = jnp.concatenate([hn_c.reshape(-1, D_MODEL), hn_l.reshape(-1, D_MODEL)], axis=0)
            f = moe(tok, *moe_params)
            h_ctx = h_ctx + mods_c[5] * f[:n_ctx_tok].reshape(h_ctx.shape)
            x = x + g2 * f[n_ctx_tok:].reshape(x.shape)
        else:
            x = x + g2 * moe(hn_l.reshape(-1, D_MODEL), *moe_params).reshape(x.shape)
    return rmsnorm(x, g_final).astype(out_dtype)
```

```python
import functools
import math

import jax
import jax.numpy as jnp
from jax import lax
from jax.experimental import pallas as pl
from jax.experimental.pallas import tpu as pltpu

F32 = jnp.float32
BF16 = jnp.bfloat16
I32 = jnp.int32
HIGHEST = lax.Precision.HIGHEST

GRID_W = 64
S5_GROUP = 16
S5_STATE = 64
S5_DT_MIN = 0.001
S5_DT_MAX = 0.1
LRU_BLOCK = 64
LRU_C = 8.0
CONV_W = 4
TOP_K = 4
SWIGLU_LIMIT = 7.0
SWIGLU_ALPHA = 1.702
EPS = 1e-6

LANES = 128
BF16_ROWS = 16
VMEM_LIMIT = 56 * 1024 * 1024

ROW_TILE = 1024
K2_TILE = 512
S5_SLAB_GROUPS = LANES // S5_GROUP
S5_SLAB_STATE = S5_SLAB_GROUPS * S5_STATE
SUB = 256
SEG_PAD = BF16_ROWS
SLOT_ROWS = 1536
EXPERT_TILE = 512


def _cparams(*sem):
    return pltpu.CompilerParams(dimension_semantics=sem, vmem_limit_bytes=VMEM_LIMIT)


def _gelu(x):
    return 0.5 * x * (1.0 + jnp.tanh(math.sqrt(2.0 / math.pi) * (x + 0.044715 * (x * x * x))))


def _sigmoid(x):
    return 1.0 / (1.0 + jnp.exp(-x))


def _rms_mod(x, g, sh, sc, nb):
    tm, d = x.shape
    y = x * lax.rsqrt(jnp.mean(x * x, axis=-1, keepdims=True) + EPS) * g
    y3 = y.reshape(tm // nb, nb, d) * (1.0 + sc)[None] + sh[None]
    return y3.reshape(tm, d)


def _ada_kernel(c_ref, w_ref, b_ref, o_ref):
    c = c_ref[...]
    cond = c * _sigmoid(c)
    o_ref[0] = jnp.dot(cond, w_ref[0], preferred_element_type=F32, precision=HIGHEST) + b_ref[0]


def _ada_all(cvec, w_ada, b_ada):
    nl, d, w6 = w_ada.shape
    r = cvec.shape[0]
    tn = min(w6, 1536)
    return pl.pallas_call(
        _ada_kernel,
        grid=(nl, w6 // tn),
        in_specs=[pl.BlockSpec((r, d), lambda l, j: (0, 0)),
                  pl.BlockSpec((1, d, tn), lambda l, j: (l, 0, j)),
                  pl.BlockSpec((1, 1, tn), lambda l, j: (l, 0, j))],
        out_specs=pl.BlockSpec((1, r, tn), lambda l, j: (l, 0, j)),
        out_shape=jax.ShapeDtypeStruct((nl, r, w6), F32),
        compiler_params=_cparams("arbitrary", "arbitrary"),
        name="ada",
    )(cvec, w_ada, b_ada.reshape(nl, 1, w6))


def _k1(x2, g, sh, sc, w_in, *, nb, r, colmajor, s5w, lw, need_gates):
    n, d = x2.shape
    t = n // nb
    ncol = t // r
    if colmajor:
        tm = (t // r) * nb
        xr_spec = pl.BlockSpec((ncol, 1, nb, lw), lambda i: (0, i, 0, 0))
    else:
        cbk = max(1, min(ncol, ROW_TILE // (r * nb)))
        tm = cbk * r * nb
        xr_spec = pl.BlockSpec((cbk, r, nb, lw), lambda i: (i, 0, 0, 0))
    splits = (s5w, s5w + lw, s5w + 2 * lw, s5w + 2 * lw + d, s5w + 2 * lw + 2 * d)
    if not need_gates:
        splits = splits[:2]
    row = lambda w: pl.BlockSpec((tm, w), lambda i: (i, 0))
    full = lambda a: pl.BlockSpec(a.shape, lambda i: (0,) * a.ndim)
    out_specs = [row(s5w), xr_spec, row(lw), row(d), row(d)]
    out_shape = [jax.ShapeDtypeStruct((n, s5w), F32),
                 jax.ShapeDtypeStruct((ncol, r, nb, lw), F32),
                 jax.ShapeDtypeStruct((n, lw), BF16),
                 jax.ShapeDtypeStruct((n, d), BF16),
                 jax.ShapeDtypeStruct((n, d), BF16)]
    k = len(splits)

    def body(x_ref, g_ref, sh_ref, sc_ref, w_ref, *o_refs):
        hn = _rms_mod(x_ref[...], g_ref[...], sh_ref[...], sc_ref[...], nb).astype(BF16)
        lo = 0
        for o_ref, hi in zip(o_refs, splits):
            v = jnp.dot(hn, w_ref[:, lo:hi], preferred_element_type=F32).astype(o_ref.dtype)
            o_ref[...] = v.reshape(o_ref.shape)
            lo = hi

    w_used = w_in[:, :splits[-1]]
    return pl.pallas_call(
        body,
        grid=(n // tm,),
        in_specs=[row(d), full(g), full(sh), full(sc), full(w_used)],
        out_specs=out_specs[:k],
        out_shape=out_shape[:k],
        compiler_params=_cparams("arbitrary"),
        name="k1_in_proj",
    )(x2, g, sh, sc, w_used)


def _s5_kernel(u_ref, bm_ref, cm_ref, are_ref, aim_ref, h0_ref, y_ref, ht_ref, hs_ref, st_ref,
               *, tt, nb, nslab):
    d = pl.program_id(0)
    j = pl.program_id(1)
    nj = pl.num_programs(1)
    sw = S5_SLAB_STATE
    rows = tt * nb

    @pl.when(j == 0)
    def _():
        st_ref[...] = h0_ref[0]

    for m in range(nslab):
        ub = u_ref[:, :, m * LANES:(m + 1) * LANES].reshape(rows, LANES).astype(BF16)
        hs_ref[...] = jnp.dot(ub, bm_ref[0, m], preferred_element_type=F32)
        a_re = jnp.broadcast_to(are_ref[0, m], (nb, sw))
        a_im = jnp.broadcast_to(aim_ref[0, m], (nb, sw))

        def step(i, carry):
            hr, hi = carry
            t = jnp.where(d == 0, i, tt - 1 - i)
            r0 = pl.multiple_of(t * nb, nb)
            bu = hs_ref[pl.ds(r0, nb), :]
            nr = a_re * hr - a_im * hi + bu[:, :sw]
            ni = a_re * hi + a_im * hr + bu[:, sw:]
            hs_ref[pl.ds(r0, nb), :] = jnp.concatenate([nr, ni], axis=-1)
            return nr, ni

        st = st_ref[m]
        hr, hi = lax.fori_loop(0, tt, step, (st[:, :sw], st[:, sw:]))
        st_ref[m] = jnp.concatenate([hr, hi], axis=-1)
        y = jnp.dot(hs_ref[...].astype(BF16), cm_ref[0, m], preferred_element_type=F32)
        y_ref[0, :, :, m * LANES:(m + 1) * LANES] = y.reshape(tt, nb, LANES)

    @pl.when(j == nj - 1)
    def _():
        ht_ref[0] = st_ref[...]


def _s5_scan(u3, bm, cm, a_re, a_im, h0):
    t, nb, s5w = u3.shape
    nslab = s5w // LANES
    tt = 64 if t % 64 == 0 else t
    nj = t // tt
    sw2 = 2 * S5_SLAB_STATE
    blk = lambda d, j: j + d * (nj - 1 - 2 * j)
    return pl.pallas_call(
        functools.partial(_s5_kernel, tt=tt, nb=nb, nslab=nslab),
        grid=(2, nj),
        in_specs=[pl.BlockSpec((tt, nb, s5w), lambda d, j: (blk(d, j), 0, 0)),
                  pl.BlockSpec((1, nslab, LANES, sw2), lambda d, j: (d, 0, 0, 0)),
                  pl.BlockSpec((1, nslab, sw2, LANES), lambda d, j: (d, 0, 0, 0)),
                  pl.BlockSpec((1, nslab, 1, S5_SLAB_STATE), lambda d, j: (d, 0, 0, 0)),
                  pl.BlockSpec((1, nslab, 1, S5_SLAB_STATE), lambda d, j: (d, 0, 0, 0)),
                  pl.BlockSpec((1, nslab, nb, sw2), lambda d, j: (d, 0, 0, 0))],
        out_specs=[pl.BlockSpec((1, tt, nb, s5w), lambda d, j: (d, blk(d, j), 0, 0)),
                   pl.BlockSpec((1, nslab, nb, sw2), lambda d, j: (d, 0, 0, 0))],
        out_shape=[jax.ShapeDtypeStruct((2, t, nb, s5w), F32),
                   jax.ShapeDtypeStruct((2, nslab, nb, sw2), F32)],
        scratch_shapes=[pltpu.VMEM((tt * nb, sw2), F32),
                        pltpu.VMEM((nslab, nb, sw2), F32)],
        compiler_params=_cparams("arbitrary", "arbitrary"),
        name="s5_scan",
    )(u3, bm, cm, a_re, a_im, h0)


def _s5_operators(lam_re, lam_im, log_dt, b_re, b_im, c_re, c_im):
    lr, li = lam_re.astype(F32), lam_im.astype(F32)
    dt = jnp.exp(log_dt.astype(F32))[..., None]
    mag = jnp.exp(lr * dt)
    ab_re, ab_im = mag * jnp.cos(li * dt), mag * jnp.sin(li * dt)
    den = lr * lr + li * li
    zr = ab_re - 1.0
    q_re = (zr * lr + ab_im * li) / den
    q_im = (ab_im * lr - zr * li) / den
    br, bi = b_re.astype(F32), b_im.astype(F32)
    bb_re = q_re[..., None] * br - q_im[..., None] * bi
    bb_im = q_re[..., None] * bi + q_im[..., None] * br
    two, g, p, h = bb_re.shape
    gs = S5_SLAB_GROUPS
    nslab = g // gs
    eye = jnp.eye(gs, dtype=F32)

    def in_mat(bb):
        bb = bb.reshape(two, nslab, gs, p, h)
        m = jnp.einsum('dmgph,gk->dmghkp', bb, eye)
        return m.reshape(two, nslab, gs * h, gs * p)

    def out_mat(cc):
        cc = cc.astype(F32).reshape(two, nslab, gs, h, p)
        m = jnp.einsum('dmghp,gk->dmgpkh', cc, eye)
        return m.reshape(two, nslab, gs * p, gs * h)

    bm = jnp.concatenate([in_mat(bb_re), in_mat(bb_im)], axis=-1).astype(BF16)
    cm = jnp.concatenate([out_mat(c_re), -out_mat(c_im)], axis=-2).astype(BF16)
    a_re = ab_re.reshape(two, nslab, 1, gs * p)
    a_im = ab_im.reshape(two, nslab, 1, gs * p)
    return bm, cm, a_re, a_im


def _lru_kernel(x_ref, xp_ref, xn_ref, cw_ref, cb_ref, sp_ref, wg_ref, ba_ref, bx_ref, h0_ref,
                y_ref, ht_ref, xpad_ref, xc_ref, a_ref, b_ref, st_ref, *, cb, r, nb):
    d = pl.program_id(0)
    j = pl.program_id(1)
    nj = pl.num_programs(1)
    jeff = j + d * (nj - 1 - 2 * j)
    steps = cb * r
    rows = steps * nb
    w = x_ref.shape[-1]

    @pl.when(j == 0)
    def _():
        st_ref[...] = h0_ref[0]

    has_prev = (jeff > 0).astype(F32)
    has_next = (jeff < nj - 1).astype(F32)
    xpad_ref[0:nb, :] = xp_ref[0, 0].astype(F32) * has_prev
    xpad_ref[nb:nb + rows, :] = x_ref[...].reshape(rows, w).astype(F32)
    xpad_ref[nb + rows:3 * nb + rows, :] = xn_ref[0].reshape(2 * nb, w).astype(F32) * has_next
    acc = cb_ref[...] + cw_ref[0:1, :] * xpad_ref[0:rows, :]
    for k in range(1, CONV_W):
        acc = acc + cw_ref[k:k + 1, :] * xpad_ref[k * nb:k * nb + rows, :]
    xc_ref[...] = acc

    for s in range(w // LANES):
        ls = slice(s * LANES, (s + 1) * LANES)
        xs = xc_ref[:, ls]
        gs = jnp.dot(xs.astype(BF16), wg_ref[0, s], preferred_element_type=F32)
        rg = _sigmoid(gs[:, :LANES] + ba_ref[0, :, ls])
        ig = _sigmoid(gs[:, LANES:] + bx_ref[0, :, ls])
        log_a = (-LRU_C) * rg * sp_ref[0, :, ls]
        a_ref[:, ls] = jnp.exp(log_a)
        b_ref[:, ls] = jnp.sqrt(1.0 - jnp.exp(2.0 * log_a)) * (ig * xs)

    def step(i, h):
        t = jnp.where(d == 0, i, steps - 1 - i)
        r0 = pl.multiple_of(t * nb, nb)
        h = a_ref[pl.ds(r0, nb), :] * h + b_ref[pl.ds(r0, nb), :]
        y_ref[0, t // r, t % r] = h
        return h

    st_ref[...] = lax.fori_loop(0, steps, step, st_ref[...])

    @pl.when(j == nj - 1)
    def _():
        ht_ref[0] = st_ref[...]


def _lru_scan(x4, conv_w, conv_b, sp, wg, b_a, b_x, h0, *, cb):
    c, r, nb, w = x4.shape
    nj = c // cb
    rows = cb * r * nb
    blk = lambda d, j: j + d * (nj - 1 - 2 * j)
    full2 = lambda a: pl.BlockSpec(a.shape, lambda d, j: (0,) * a.ndim)
    perdir = lambda a: pl.BlockSpec((1,) + a.shape[1:], lambda d, j: (d,) + (0,) * (a.ndim - 1))
    return pl.pallas_call(
        functools.partial(_lru_kernel, cb=cb, r=r, nb=nb),
        grid=(2, nj),
        in_specs=[pl.BlockSpec((cb, r, nb, w), lambda d, j: (blk(d, j), 0, 0, 0)),
                  pl.BlockSpec((1, 1, nb, w),
                               lambda d, j: (jnp.maximum(blk(d, j) * cb - 1, 0), r - 1, 0, 0)),
                  pl.BlockSpec((1, 2, nb, w),
                               lambda d, j: (jnp.minimum((blk(d, j) + 1) * cb, c - 1), 0, 0, 0)),
                  full2(conv_w), full2(conv_b), perdir(sp), perdir(wg), perdir(b_a), perdir(b_x),
                  perdir(h0)],
        out_specs=[pl.BlockSpec((1, cb, r, nb, w), lambda d, j: (d, blk(d, j), 0, 0, 0)),
                   pl.BlockSpec((1, nb, w), lambda d, j: (d, 0, 0))],
        out_shape=[jax.ShapeDtypeStruct((2, c, r, nb, w), F32),
                   jax.ShapeDtypeStruct((2, nb, w), F32)],
        scratch_shapes=[pltpu.VMEM((rows + 3 * nb, w), F32),
                        pltpu.VMEM((rows, w), F32),
                        pltpu.VMEM((rows, w), F32),
                        pltpu.VMEM((rows, w), F32),
                        pltpu.VMEM((nb, w), F32)],
        compiler_params=_cparams("arbitrary", "arbitrary"),
        name="lru_scan",
    )(x4, x4, x4, conv_w, conv_b, sp, wg, b_a, b_x, h0)


def _lru_gate_weights(w_a, w_x):
    two, nblk, k, _ = w_a.shape
    per = LANES // k
    eye = jnp.eye(per, dtype=F32)

    def slabs(wm):
        wm = wm.astype(F32).reshape(two, nblk // per, per, k, k)
        m = jnp.einsum('dsgij,gk->dsgikj', wm, eye)
        return m.reshape(two, nblk // per, LANES, LANES)

    return jnp.concatenate([slabs(w_a), slabs(w_x)], axis=-1).astype(BF16)


def _k2_kernel(x_ref, u_ref, ys_ref, yl_ref, gg_ref, m5_ref, ml_ref, d5_ref, wglu_ref, bglu_ref,
               wp5_ref, wpl_ref, wo_ref, g1_ref, gf_ref, sh_ref, sc_ref, wr_ref, br_ref,
               x1_ref, hn_ref, ti_ref, tw_ref, cnt_ref, *, nb):
    tm, dm = x_ref.shape
    u = u_ref[...]
    y5 = ys_ref[0] + ys_ref[1] + d5_ref[...] * u
    gy = _gelu(y5)
    o5 = gy * _sigmoid(jnp.dot(gy.astype(BF16), wglu_ref[...], preferred_element_type=F32)
                       + bglu_ref[...])
    yl = (yl_ref[0] + yl_ref[1]).reshape(tm, -1)
    ol = _gelu(gg_ref[...].astype(F32)) * yl
    merged = (_sigmoid(m5_ref[...].astype(F32))
              * jnp.dot(o5.astype(BF16), wp5_ref[...], preferred_element_type=F32)
              + _sigmoid(ml_ref[...].astype(F32))
              * jnp.dot(ol.astype(BF16), wpl_ref[...], preferred_element_type=F32))
    mix = jnp.dot(merged.astype(BF16), wo_ref[...], preferred_element_type=F32)
    x1 = (x_ref[...].reshape(tm // nb, nb, dm) + g1_ref[...][None] * mix.reshape(tm // nb, nb, dm))
    x1 = x1.reshape(tm, dm)
    x1_ref[...] = x1
    hn = _rms_mod(x1, gf_ref[...], sh_ref[...], sc_ref[...], nb)
    hn_ref[...] = hn.astype(BF16)

    logits = lax.dot_general(wr_ref[...], hn, (((1,), (1,)), ((), ())),
                             preferred_element_type=F32, precision=HIGHEST) + br_ref[...]
    ne = logits.shape[0]
    eidx = lax.broadcasted_iota(I32, (ne, tm), 0).astype(F32)
    vals = logits
    tv, ti = [], []
    for _ in range(TOP_K):
        m = jnp.max(vals, axis=0, keepdims=True)
        idx = jnp.min(jnp.where(vals == m, eidx, float(ne)), axis=0, keepdims=True)
        tv.append(m)
        ti.append(idx)
        vals = jnp.where(eidx == idx, -jnp.inf, vals)
    ex = [jnp.exp(v - tv[0]) for v in tv]
    den = ex[0] + ex[1] + ex[2] + ex[3]
    ti_ref[...] = jnp.concatenate(ti, axis=0).astype(I32)
    tw_ref[...] = jnp.concatenate([e / den for e in ex], axis=0)
    sel = (eidx == ti[0]) | (eidx == ti[1]) | (eidx == ti[2]) | (eidx == ti[3])
    self32 = jnp.where(sel, 1.0, 0.0)
    for sb in range(tm // SUB):
        cnt_ref[sb] = jnp.sum(self32[:, sb * SUB:(sb + 1) * SUB], axis=1, keepdims=True).astype(I32)


def _k2(x2, u, ys, yl5, gg, m5, ml, d5, wglu, bglu, wp5, wpl, wo, g1, gf, sh, sc, wr_t, br,
        *, nb, r, colmajor):
    n, dm = x2.shape
    t = n // nb
    ncol = t // r
    lw = yl5.shape[-1]
    s5w = u.shape[-1]
    ne = wr_t.shape[0]
    if colmajor:
        cbk = max(1, min(ncol, K2_TILE // nb))
        per_row = ncol // cbk
        tm = cbk * nb
        yl_spec = pl.BlockSpec((2, cbk, 1, nb, lw),
                               lambda i: (0, i % per_row, i // per_row, 0, 0))
    else:
        cbk = max(1, min(ncol, K2_TILE // (r * nb)))
        tm = cbk * r * nb
        yl_spec = pl.BlockSpec((2, cbk, r, nb, lw), lambda i: (0, i, 0, 0, 0))
    row = lambda w: pl.BlockSpec((tm, w), lambda i: (i, 0))
    full = lambda a: pl.BlockSpec(a.shape, lambda i: (0,) * a.ndim)
    ys2 = ys.reshape(2, n, s5w)
    params = (d5, wglu, bglu, wp5, wpl, wo, g1, gf, sh, sc, wr_t, br)
    return pl.pallas_call(
        functools.partial(_k2_kernel, nb=nb),
        grid=(n // tm,),
        in_specs=[row(dm), row(s5w), pl.BlockSpec((2, tm, s5w), lambda i: (0, i, 0)), yl_spec,
                  row(lw), row(dm), row(dm)] + [full(p) for p in params],
        out_specs=[row(dm), row(dm),
                   pl.BlockSpec((TOP_K, tm), lambda i: (0, i)),
                   pl.BlockSpec((TOP_K, tm), lambda i: (0, i)),
                   pl.BlockSpec((tm // SUB, ne, 1), lambda i: (i, 0, 0))],
        out_shape=[jax.ShapeDtypeStruct((n, dm), F32),
                   jax.ShapeDtypeStruct((n, dm), BF16),
                   jax.ShapeDtypeStruct((TOP_K, n), I32),
                   jax.ShapeDtypeStruct((TOP_K, n), F32),
                   jax.ShapeDtypeStruct((n // SUB, ne, 1), I32)],
        compiler_params=_cparams("arbitrary"),
        name="k2_mix_out",
    )(x2, u, ys2, yl5, gg, m5, ml, *params)


def _seg_copies(n_seg, pcs, src_off, dst_off, src_ref, dst_ref, sem, base):
    def per_seg(e, _):
        n = pcs[base + e] // SEG_PAD
        so = src_off[base + e]
        do = dst_off[base + e]

        def per_chunk(c, _):
            s0 = pl.multiple_of(so + c * SEG_PAD, SEG_PAD)
            d0 = pl.multiple_of(do + c * SEG_PAD, SEG_PAD)
            pltpu.make_async_copy(src_ref.at[pl.ds(s0, SEG_PAD)], dst_ref.at[pl.ds(d0, SEG_PAD)],
                                  sem).start()
            return 0

        lax.fori_loop(0, n, per_chunk, 0)
        return 0

    lax.fori_loop(0, n_seg, per_seg, 0)


def _wait_chunks(n, src_ref, dst_ref, sem):
    def one(c, _):
        pltpu.make_async_copy(src_ref.at[pl.ds(0, SEG_PAD)], dst_ref.at[pl.ds(0, SEG_PAD)],
                              sem).wait()
        return 0

    lax.fori_loop(0, n, one, 0)


def _dispatch_kernel(pc_ref, loc_ref, glob_ref, nch_ref, tail0_ref, tailn_ref,
                     hn_ref, ti_ref, locv_ref, xs_ref, slot_ref, xg_ref, zero_ref, sem, tsem,
                     *, ne):
    s = pl.program_id(0)
    ns = pl.num_programs(0)
    buf = s % 2

    @pl.when(s == 0)
    def _():
        zero_ref[...] = jnp.zeros_like(zero_ref)

        def per_e(e, _):
            def per_chunk(c, _):
                d0 = pl.multiple_of(tail0_ref[e] + c * SEG_PAD, SEG_PAD)
                pltpu.make_async_copy(zero_ref, xs_ref.at[pl.ds(d0, SEG_PAD)], tsem).start()
                return 0
            lax.fori_loop(0, tailn_ref[e], per_chunk, 0)
            return 0
        lax.fori_loop(0, ne, per_e, 0)

        def per_e_wait(e, _):
            def per_chunk(c, _):
                pltpu.make_async_copy(zero_ref, xs_ref.at[pl.ds(0, SEG_PAD)], tsem).wait()
                return 0
            lax.fori_loop(0, tailn_ref[e], per_chunk, 0)
            return 0
        lax.fori_loop(0, ne, per_e_wait, 0)

    ti = ti_ref[...].astype(F32)
    eidx = lax.broadcasted_iota(I32, (ne, SUB), 0).astype(F32)
    hit = [eidx == ti[k:k + 1] for k in range(TOP_K)]
    sel = hit[0] | hit[1] | hit[2] | hit[3]
    upper = (lax.broadcasted_iota(I32, (SUB, SUB), 0)
             < lax.broadcasted_iota(I32, (SUB, SUB), 1))
    rank = jnp.dot(jnp.where(sel, 1.0, 0.0).astype(BF16), jnp.where(upper, 1.0, 0.0).astype(BF16),
                   preferred_element_type=F32)
    slot_e = locv_ref[0].astype(F32) + rank
    slots = [jnp.sum(jnp.where(hit[k], slot_e, 0.0), axis=0, keepdims=True) for k in range(TOP_K)]
    slot_ref[...] = jnp.concatenate(slots, axis=0).astype(I32)

    hn = hn_ref[...]
    blk = 256
    for rb in range(SLOT_ROWS // blk):
        sidx = (lax.broadcasted_iota(I32, (blk, SUB), 0) + rb * blk).astype(F32)
        p = (sidx == slots[0]) | (sidx == slots[1]) | (sidx == slots[2]) | (sidx == slots[3])
        xg = jnp.dot(jnp.where(p, 1.0, 0.0).astype(BF16), hn, preferred_element_type=F32)
        xg_ref[buf, rb * blk:(rb + 1) * blk, :] = xg.astype(BF16)

    _seg_copies(ne, pc_ref, loc_ref, glob_ref, xg_ref.at[buf], xs_ref, sem.at[buf], s * ne)

    @pl.when(s > 0)
    def _():
        _wait_chunks(nch_ref[s - 1], xg_ref.at[1 - buf], xs_ref, sem.at[1 - buf])

    @pl.when(s == ns - 1)
    def _():
        _wait_chunks(nch_ref[s], xg_ref.at[buf], xs_ref, sem.at[buf])


def _dispatch(hn, ti, meta, *, ne):
    n, dm = hn.shape
    ns = n // SUB
    gs = pltpu.PrefetchScalarGridSpec(
        num_scalar_prefetch=6,
        grid=(ns,),
        in_specs=[pl.BlockSpec((SUB, dm), lambda s, *_: (s, 0)),
                  pl.BlockSpec((TOP_K, SUB), lambda s, *_: (0, s)),
                  pl.BlockSpec((1, ne, 1), lambda s, *_: (s, 0, 0))],
        out_specs=[pl.BlockSpec(memory_space=pl.ANY),
                   pl.BlockSpec((TOP_K, SUB), lambda s, *_: (0, s))],
        scratch_shapes=[pltpu.VMEM((2, SLOT_ROWS, dm), BF16),
                        pltpu.VMEM((SEG_PAD, dm), BF16),
                        pltpu.SemaphoreType.DMA((2,)),
                        pltpu.SemaphoreType.DMA(())],
    )
    return pl.pallas_call(
        functools.partial(_dispatch_kernel, ne=ne),
        grid_spec=gs,
        out_shape=[jax.ShapeDtypeStruct((meta['p_max'], dm), BF16),
                   jax.ShapeDtypeStruct((TOP_K, n), I32)],
        compiler_params=_cparams("arbitrary"),
        name="moe_dispatch",
    )(meta['pc'], meta['loc'], meta['glob'], meta['nch'], meta['tail0'], meta['tailn'],
      hn, ti, meta['locv'])


def _expert_kernel(te_ref, nt_ref, x_ref, wg_ref, wu_ref, bg_ref, bu_ref, wd_ref, bd_ref, y_ref):
    i = pl.program_id(0)

    @pl.when(i < nt_ref[0])
    def _():
        x = x_ref[...]
        hg = jnp.dot(x, wg_ref[0], preferred_element_type=F32) + bg_ref[0]
        hu = jnp.dot(x, wu_ref[0], preferred_element_type=F32) + bu_ref[0]
        gate = jnp.minimum(hg, SWIGLU_LIMIT)
        up = jnp.clip(hu, -SWIGLU_LIMIT, SWIGLU_LIMIT)
        act = gate * _sigmoid(SWIGLU_ALPHA * gate) * (up + 1.0)
        y = jnp.dot(act.astype(BF16), wd_ref[0], preferred_element_type=F32) + bd_ref[0]
        y_ref[...] = y.astype(y_ref.dtype)


def _experts(xs, wg, wu, bg, bu, wd, bd, meta):
    p_max, dm = xs.shape
    ne, _, f = wg.shape
    n_tiles = p_max // EXPERT_TILE
    tile = lambda i, te, nt: (jnp.minimum(i, nt[0] - 1), 0)
    byexp = lambda i, te, nt: (te[i], 0, 0)
    gs = pltpu.PrefetchScalarGridSpec(
        num_scalar_prefetch=2,
        grid=(n_tiles,),
        in_specs=[pl.BlockSpec((EXPERT_TILE, dm), tile),
                  pl.BlockSpec((1, dm, f), byexp), pl.BlockSpec((1, dm, f), byexp),
                  pl.BlockSpec((1, 1, f), byexp), pl.BlockSpec((1, 1, f), byexp),
                  pl.BlockSpec((1, f, dm), byexp), pl.BlockSpec((1, 1, dm), byexp)],
        out_specs=pl.BlockSpec((EXPERT_TILE, dm), tile),
    )
    return pl.pallas_call(
        _expert_kernel,
        grid_spec=gs,
        out_shape=jax.ShapeDtypeStruct((p_max, dm), BF16),
        compiler_params=_cparams("arbitrary"),
        name="moe_experts",
    )(meta['tile_e'], meta['n_tiles'], xs, wg, wu, bg, bu, wd, bd)


def _combine_kernel(pc_ref, loc_ref, glob_ref, nch_ref, x_ref, slot_ref, tw_ref, g2_ref, gfin_ref,
                    ys_ref, o_ref, yl_ref, sem, *, ne, s0, nb, final_norm):
    i = pl.program_id(0)
    ni = pl.num_programs(0)
    buf = i % 2

    def fetch(step, b):
        _seg_copies(ne, pc_ref, glob_ref, loc_ref, ys_ref, yl_ref.at[b], sem.at[b],
                    (s0 + step) * ne)

    @pl.when(i == 0)
    def _():
        yl_ref[...] = jnp.zeros_like(yl_ref)
        fetch(0, 0)

    @pl.when(i + 1 < ni)
    def _():
        fetch(i + 1, 1 - buf)

    _wait_chunks(nch_ref[s0 + i], ys_ref, yl_ref.at[buf], sem.at[buf])

    slot = slot_ref[...].astype(F32)
    tw = tw_ref[...]
    sidx = lax.broadcasted_iota(I32, (SUB, SLOT_ROWS), 1).astype(F32)
    pw = jnp.where(sidx == slot[:, 0:1], tw[:, 0:1], 0.0)
    for k in range(1, TOP_K):
        pw = pw + jnp.where(sidx == slot[:, k:k + 1], tw[:, k:k + 1], 0.0)
    f = jnp.dot(pw.astype(BF16), yl_ref[buf], preferred_element_type=F32)
    dm = f.shape[-1]
    x2 = x_ref[...].reshape(SUB // nb, nb, dm) + g2_ref[...][None] * f.reshape(SUB // nb, nb, dm)
    x2 = x2.reshape(SUB, dm)
    if final_norm:
        x2 = x2 * lax.rsqrt(jnp.mean(x2 * x2, axis=-1, keepdims=True) + EPS) * gfin_ref[...]
    o_ref[...] = x2


def _combine(x2d, slot_t, tw_t, g2, gfin, ys, meta, *, ne, s0, nb, final_norm):
    n, dm = x2d.shape
    ns = n // SUB
    gs = pltpu.PrefetchScalarGridSpec(
        num_scalar_prefetch=4,
        grid=(ns,),
        in_specs=[pl.BlockSpec((SUB, dm), lambda i, *_: (i, 0)),
                  pl.BlockSpec((SUB, TOP_K), lambda i, *_: (s0 + i, 0)),
                  pl.BlockSpec((SUB, TOP_K), lambda i, *_: (s0 + i, 0)),
                  pl.BlockSpec(g2.shape, lambda i, *_: (0, 0)),
                  pl.BlockSpec(gfin.shape, lambda i, *_: (0, 0)),
                  pl.BlockSpec(memory_space=pl.ANY)],
        out_specs=pl.BlockSpec((SUB, dm), lambda i, *_: (i, 0)),
        scratch_shapes=[pltpu.VMEM((2, SLOT_ROWS, dm), BF16),
                        pltpu.SemaphoreType.DMA((2,))],
    )
    return pl.pallas_call(
        functools.partial(_combine_kernel, ne=ne, s0=s0, nb=nb, final_norm=final_norm),
        grid_spec=gs,
        out_shape=jax.ShapeDtypeStruct((n, dm), F32),
        compiler_params=_cparams("arbitrary"),
        name="moe_combine",
    )(meta['pc'], meta['loc'], meta['glob'], meta['nch'], x2d, slot_t, tw_t, g2, gfin, ys)


def _routing_meta(cnt, ne):
    ns = cnt.shape[0]
    pc = (cnt + (SEG_PAD - 1)) // SEG_PAD * SEG_PAD
    loc = jnp.cumsum(pc, axis=1) - pc
    tot = jnp.sum(pc, axis=0)
    reg = (tot + (EXPERT_TILE - 1)) // EXPERT_TILE * EXPERT_TILE
    reg_end = jnp.cumsum(reg)
    base = reg_end - reg
    glob = base[None, :] + jnp.cumsum(pc, axis=0) - pc
    p_max = ns * (SUB * TOP_K + ne * (SEG_PAD - 1)) + ne * (EXPERT_TILE - SEG_PAD)
    p_max = (p_max + EXPERT_TILE - 1) // EXPERT_TILE * EXPERT_TILE
    n_tiles_max = p_max // EXPERT_TILE
    tile_start = jnp.arange(n_tiles_max, dtype=I32) * EXPERT_TILE
    tile_e = jnp.minimum(jnp.searchsorted(reg_end, tile_start, side='right'), ne - 1).astype(I32)
    return dict(pc=pc.reshape(-1).astype(I32), loc=loc.reshape(-1).astype(I32),
                glob=glob.reshape(-1).astype(I32),
                nch=(jnp.sum(pc, axis=1) // SEG_PAD).astype(I32),
                tail0=(base + tot).astype(I32), tailn=((reg - tot) // SEG_PAD).astype(I32),
                locv=loc.reshape(ns, ne, 1).astype(I32),
                tile_e=tile_e, n_tiles=(reg_end[-1:] // EXPERT_TILE).astype(I32), p_max=p_max)


def kernel(x, c, ctx, c_ctx, w_ada, b_ada, g_mix, g_ffn, w_in, s5_lam_re, s5_lam_im, s5_log_dt, s5_b_re, s5_b_im, s5_c_re, s5_c_im, s5_d, s5_w_glu, s5_b_glu, lru_conv_w, lru_conv_b, lru_lam, lru_w_a, lru_b_a, lru_w_x, lru_b_x, w_proj_s5, w_proj_lru, w_out, w_router, b_router, w_up, b_up, w_down, b_down, g_final):
    nb, seq, dm = x.shape
    tc = ctx.shape[1]
    depth = w_ada.shape[0]
    r = seq // GRID_W
    s5w = s5_d.shape[-1]
    lw = lru_conv_w.shape[-1]
    ne = w_router.shape[-1]
    nl, ncx = seq * nb, tc * nb

    xl = jnp.transpose(x, (1, 0, 2)).reshape(nl, dm)
    xc = jnp.transpose(ctx, (1, 0, 2)).reshape(ncx, dm)

    pad_rows = (-(nb + 1)) % 8
    cvec = jnp.concatenate([c, c_ctx[None, :], jnp.zeros((pad_rows, dm), F32)], axis=0)
    ada = _ada_all(cvec, w_ada, b_ada)

    for l in range(depth):
        need_ctx = l < depth - 1
        mods_l = [ada[l, :nb, k * dm:(k + 1) * dm] for k in range(6)]
        mods_c = [jnp.broadcast_to(ada[l, nb:nb + 1, k * dm:(k + 1) * dm], (nb, dm)) for k in range(6)]
        gm = g_mix[l][None, :]
        gf = g_ffn[l][None, :]
        w_in_l = w_in[l].astype(BF16)

        u_l, xr_l, gg_l, m5_l, ml_l = _k1(xl, gm, mods_l[0], mods_l[1], w_in_l, nb=nb, r=r,
                                          colmajor=True, s5w=s5w, lw=lw, need_gates=True)
        outs_c = _k1(xc, gm, mods_c[0], mods_c[1], w_in_l, nb=nb, r=r, colmajor=False,
                     s5w=s5w, lw=lw, need_gates=need_ctx)
        u_c, xr_c = outs_c[0], outs_c[1]

        bm, cm, a_re, a_im = _s5_operators(s5_lam_re[l], s5_lam_im[l], s5_log_dt[l], s5_b_re[l],
                                           s5_b_im[l], s5_c_re[l], s5_c_im[l])
        nslab = s5w // LANES
        h0 = jnp.zeros((2, nslab, nb, 2 * S5_SLAB_STATE), F32)
        ys_c, hs_c = _s5_scan(u_c.reshape(tc, nb, s5w), bm, cm, a_re, a_im, h0)
        ys_l, _ = _s5_scan(u_l.reshape(seq, nb, s5w), bm, cm, a_re, a_im, hs_c)

        lam = lru_lam[l].astype(F32)
        sp = (jnp.maximum(-lam, 0.0) + jnp.log1p(jnp.exp(-jnp.abs(lam))))[:, None, :]
        wg = _lru_gate_weights(lru_w_a[l], lru_w_x[l])
        cw = lru_conv_w[l]
        cbias = lru_conv_b[l][None, :]
        b_a = lru_b_a[l][:, None, :]
        b_x = lru_b_x[l][:, None, :]
        hz = jnp.zeros((2, nb, lw), F32)
        cb_c = 2 if (tc // r) % 2 == 0 else 1
        cb_l = 2 if GRID_W % 2 == 0 else 1
        yl_c, hl_c = _lru_scan(xr_c, cw, cbias, sp, wg, b_a, b_x, hz, cb=cb_c)
        yl_l, _ = _lru_scan(xr_l, cw, cbias, sp, wg, b_a, b_x, hl_c, cb=cb_l)

        k2_params = (s5_d[l][None, :], s5_w_glu[l].astype(BF16), s5_b_glu[l][None, :],
                     w_proj_s5[l].astype(BF16), w_proj_lru[l].astype(BF16), w_out[l].astype(BF16))
        wr_t = jnp.transpose(w_router[l])
        br = b_router[l][:, None]
        x1_l, hn_l, ti_l, tw_l, cnt_l = _k2(xl, u_l, ys_l, yl_l, gg_l, m5_l, ml_l, *k2_params,
                                            mods_l[2], gf, mods_l[3], mods_l[4], wr_t, br,
                                            nb=nb, r=r, colmajor=True)
        if need_ctx:
            _, _, gg_c, m5_c, ml_c = outs_c
            x1_c, hn_c, ti_c, tw_c, cnt_c = _k2(xc, u_c, ys_c, yl_c, gg_c, m5_c, ml_c, *k2_params,
                                                mods_c[2], gf, mods_c[3], mods_c[4], wr_t, br,
                                                nb=nb, r=r, colmajor=False)
            hn = jnp.concatenate([hn_c, hn_l], axis=0)
            ti = jnp.concatenate([ti_c, ti_l], axis=1)
            tw = jnp.concatenate([tw_c, tw_l], axis=1)
            cnt = jnp.concatenate([cnt_c, cnt_l], axis=0)
        else:
            hn, ti, tw, cnt = hn_l, ti_l, tw_l, cnt_l

        meta = _routing_meta(cnt[:, :, 0], ne)
        xs, slots = _dispatch(hn, ti, meta, ne=ne)
        wu_l = w_up[l]
        ys = _experts(xs, wu_l[:, :, 0::2].astype(BF16), wu_l[:, :, 1::2].astype(BF16),
                      b_up[l][:, None, 0::2], b_up[l][:, None, 1::2],
                      w_down[l].astype(BF16), b_down[l][:, None, :], meta)
        slot_t = jnp.transpose(slots)
        tw_t = jnp.transpose(tw)
        last = l == depth - 1
        gfin = g_final[None, :]
        if need_ctx:
            xc = _combine(x1_c, slot_t, tw_t, mods_c[5], gfin, ys, meta, ne=ne, s0=0, nb=nb,
                          final_norm=False)
            xl = _combine(x1_l, slot_t, tw_t, mods_l[5], gfin, ys, meta, ne=ne, s0=ncx // SUB,
                          nb=nb, final_norm=False)
        else:
            xl = _combine(x1_l, slot_t, tw_t, mods_l[5], gfin, ys, meta, ne=ne, s0=0, nb=nb,
                          final_norm=last)

    return jnp.transpose(xl.reshape(seq, nb, dm), (1, 0, 2)).astype(x.dtype)
```

```python
import functools
import math

import jax
import jax.numpy as jnp
from jax import lax
from jax.experimental import pallas as pl
from jax.experimental.pallas import tpu as pltpu

F32 = jnp.float32
BF16 = jnp.bfloat16
I32 = jnp.int32
HIGHEST = lax.Precision.HIGHEST

GRID_W = 64
S5_GROUP = 16
S5_STATE = 64
S5_DT_MIN = 0.001
S5_DT_MAX = 0.1
LRU_BLOCK = 64
LRU_C = 8.0
CONV_W = 4
TOP_K = 4
SWIGLU_LIMIT = 7.0
SWIGLU_ALPHA = 1.702
EPS = 1e-6

LANES = 128
BF16_ROWS = 16
VMEM_LIMIT = 56 * 1024 * 1024

ROW_TILE = 1024
K2_TILE = 512
S5_SLAB_GROUPS = LANES // S5_GROUP
S5_SLAB_STATE = S5_SLAB_GROUPS * S5_STATE
SUB = 256
SEG_PAD = BF16_ROWS
SLOT_ROWS = 1536
EXPERT_TILE = 512


def _cparams(*sem, **kw):
    return pltpu.CompilerParams(dimension_semantics=sem, vmem_limit_bytes=VMEM_LIMIT, **kw)


def _gelu(x):
    return 0.5 * x * (1.0 + jnp.tanh(math.sqrt(2.0 / math.pi) * (x + 0.044715 * (x * x * x))))


def _sigmoid(x):
    return 0.5 * jnp.tanh(0.5 * x) + 0.5


def _rms_mod(x, g, sh, sc, nb):
    tm, d = x.shape
    y = x * lax.rsqrt(jnp.mean(x * x, axis=-1, keepdims=True) + EPS) * g
    y3 = y.reshape(tm // nb, nb, d) * (1.0 + sc)[None] + sh[None]
    return y3.reshape(tm, d)


def _ada_kernel(c_ref, w_ref, b_ref, o_ref):
    c = c_ref[...]
    cond = c * _sigmoid(c)
    o_ref[0] = jnp.dot(cond, w_ref[0], preferred_element_type=F32, precision=HIGHEST) + b_ref[0]


def _ada_all(cvec, w_ada, b_ada):
    nl, d, w6 = w_ada.shape
    r = cvec.shape[0]
    tn = min(w6, 1536)
    return pl.pallas_call(
        _ada_kernel,
        grid=(nl, w6 // tn),
        in_specs=[pl.BlockSpec((r, d), lambda l, j: (0, 0)),
                  pl.BlockSpec((1, d, tn), lambda l, j: (l, 0, j)),
                  pl.BlockSpec((1, 1, tn), lambda l, j: (l, 0, j))],
        out_specs=pl.BlockSpec((1, r, tn), lambda l, j: (l, 0, j)),
        out_shape=jax.ShapeDtypeStruct((nl, r, w6), F32),
        compiler_params=_cparams("arbitrary", "arbitrary"),
        name="ada",
    )(cvec, w_ada, b_ada.reshape(nl, 1, w6))


def _k1(x2, g, sh, sc, w_in, *, nb, r, colmajor, s5w, lw, need_gates):
    n, d = x2.shape
    t = n // nb
    ncol = t // r
    if colmajor:
        tm = (t // r) * nb
        xr_spec = pl.BlockSpec((ncol, 1, nb, lw), lambda i: (0, i, 0, 0))
    else:
        cbk = max(1, min(ncol, ROW_TILE // (r * nb)))
        tm = cbk * r * nb
        xr_spec = pl.BlockSpec((cbk, r, nb, lw), lambda i: (i, 0, 0, 0))
    splits = (s5w, s5w + lw, s5w + 2 * lw, s5w + 2 * lw + d, s5w + 2 * lw + 2 * d)
    if not need_gates:
        splits = splits[:2]
    row = lambda w: pl.BlockSpec((tm, w), lambda i: (i, 0))
    full = lambda a: pl.BlockSpec(a.shape, lambda i: (0,) * a.ndim)
    out_specs = [row(s5w), xr_spec, row(lw), row(d), row(d)]
    out_shape = [jax.ShapeDtypeStruct((n, s5w), F32),
                 jax.ShapeDtypeStruct((ncol, r, nb, lw), F32),
                 jax.ShapeDtypeStruct((n, lw), BF16),
                 jax.ShapeDtypeStruct((n, d), BF16),
                 jax.ShapeDtypeStruct((n, d), BF16)]
    k = len(splits)

    def body(x_ref, g_ref, sh_ref, sc_ref, w_ref, *o_refs):
        hn = _rms_mod(x_ref[...], g_ref[...], sh_ref[...], sc_ref[...], nb).astype(BF16)
        lo = 0
        for o_ref, hi in zip(o_refs, splits):
            v = jnp.dot(hn, w_ref[:, lo:hi], preferred_element_type=F32).astype(o_ref.dtype)
            o_ref[...] = v.reshape(o_ref.shape)
            lo = hi

    w_used = w_in[:, :splits[-1]]
    return pl.pallas_call(
        body,
        grid=(n // tm,),
        in_specs=[row(d), full(g), full(sh), full(sc), full(w_used)],
        out_specs=out_specs[:k],
        out_shape=out_shape[:k],
        compiler_params=_cparams("arbitrary"),
        name="k1_in_proj",
    )(x2, g, sh, sc, w_used)


def _s5_kernel(u_ref, bm_ref, cm_ref, are_ref, aim_ref, h0_ref, y_ref, ht_ref, hs_ref, st_ref,
               *, tt, nb, nslab):
    d = pl.program_id(0)
    j = pl.program_id(1)
    nj = pl.num_programs(1)
    sw = S5_SLAB_STATE
    rows = tt * nb

    @pl.when(j == 0)
    def _():
        st_ref[...] = h0_ref[0]

    def run(reverse):
        for m in range(nslab):
            ub = u_ref[:, :, m * LANES:(m + 1) * LANES].reshape(rows, LANES).astype(BF16)
            hs_ref[m] = jnp.dot(ub, bm_ref[0, m], preferred_element_type=F32)
            a_re = jnp.broadcast_to(are_ref[0, m], (nb, sw))
            a_im = jnp.broadcast_to(aim_ref[0, m], (nb, sw))
            hr = st_ref[m, :, :sw]
            hi = st_ref[m, :, sw:]
            for i in range(tt):
                t = tt - 1 - i if reverse else i
                rs = slice(t * nb, (t + 1) * nb)
                nr = a_re * hr - a_im * hi + hs_ref[m, rs, :sw]
                ni = a_re * hi + a_im * hr + hs_ref[m, rs, sw:]
                hs_ref[m, rs, :sw] = nr
                hs_ref[m, rs, sw:] = ni
                hr, hi = nr, ni
            st_ref[m, :, :sw] = hr
            st_ref[m, :, sw:] = hi
            y = jnp.dot(hs_ref[m].astype(BF16), cm_ref[0, m], preferred_element_type=F32)
            y_ref[0, :, :, m * LANES:(m + 1) * LANES] = y.reshape(tt, nb, LANES)

    @pl.when(d == 0)
    def _():
        run(False)

    @pl.when(d == 1)
    def _():
        run(True)

    @pl.when(j == nj - 1)
    def _():
        ht_ref[0] = st_ref[...]


def _s5_scan(u3, bm, cm, a_re, a_im, h0):
    t, nb, s5w = u3.shape
    nslab = s5w // LANES
    tt = 64 if t % 64 == 0 else t
    nj = t // tt
    sw2 = 2 * S5_SLAB_STATE
    blk = lambda d, j: j + d * (nj - 1 - 2 * j)
    return pl.pallas_call(
        functools.partial(_s5_kernel, tt=tt, nb=nb, nslab=nslab),
        grid=(2, nj),
        in_specs=[pl.BlockSpec((tt, nb, s5w), lambda d, j: (blk(d, j), 0, 0)),
                  pl.BlockSpec((1, nslab, LANES, sw2), lambda d, j: (d, 0, 0, 0)),
                  pl.BlockSpec((1, nslab, sw2, LANES), lambda d, j: (d, 0, 0, 0)),
                  pl.BlockSpec((1, nslab, 1, S5_SLAB_STATE), lambda d, j: (d, 0, 0, 0)),
                  pl.BlockSpec((1, nslab, 1, S5_SLAB_STATE), lambda d, j: (d, 0, 0, 0)),
                  pl.BlockSpec((1, nslab, nb, sw2), lambda d, j: (d, 0, 0, 0))],
        out_specs=[pl.BlockSpec((1, tt, nb, s5w), lambda d, j: (d, blk(d, j), 0, 0)),
                   pl.BlockSpec((1, nslab, nb, sw2), lambda d, j: (d, 0, 0, 0))],
        out_shape=[jax.ShapeDtypeStruct((2, t, nb, s5w), F32),
                   jax.ShapeDtypeStruct((2, nslab, nb, sw2), F32)],
        scratch_shapes=[pltpu.VMEM((nslab, tt * nb, sw2), F32),
                        pltpu.VMEM((nslab, nb, sw2), F32)],
        compiler_params=_cparams("arbitrary", "arbitrary"),
        name="s5_scan",
    )(u3, bm, cm, a_re, a_im, h0)


def _s5_operators(lam_re, lam_im, log_dt, b_re, b_im, c_re, c_im):
    lr, li = lam_re.astype(F32), lam_im.astype(F32)
    dt = jnp.exp(log_dt.astype(F32))[..., None]
    mag = jnp.exp(lr * dt)
    ab_re, ab_im = mag * jnp.cos(li * dt), mag * jnp.sin(li * dt)
    den = lr * lr + li * li
    zr = ab_re - 1.0
    q_re = (zr * lr + ab_im * li) / den
    q_im = (ab_im * lr - zr * li) / den
    br, bi = b_re.astype(F32), b_im.astype(F32)
    bb_re = q_re[..., None] * br - q_im[..., None] * bi
    bb_im = q_re[..., None] * bi + q_im[..., None] * br
    two, g, p, h = bb_re.shape
    gs = S5_SLAB_GROUPS
    nslab = g // gs
    eye = jnp.eye(gs, dtype=F32)

    def in_mat(bb):
        bb = bb.reshape(two, nslab, gs, p, h)
        m = jnp.einsum('dmgph,gk->dmghkp', bb, eye)
        return m.reshape(two, nslab, gs * h, gs * p)

    def out_mat(cc):
        cc = cc.astype(F32).reshape(two, nslab, gs, h, p)
        m = jnp.einsum('dmghp,gk->dmgpkh', cc, eye)
        return m.reshape(two, nslab, gs * p, gs * h)

    bm = jnp.concatenate([in_mat(bb_re), in_mat(bb_im)], axis=-1).astype(BF16)
    cm = jnp.concatenate([out_mat(c_re), -out_mat(c_im)], axis=-2).astype(BF16)
    a_re = ab_re.reshape(two, nslab, 1, gs * p)
    a_im = ab_im.reshape(two, nslab, 1, gs * p)
    return bm, cm, a_re, a_im


def _lru_kernel(x_ref, xp_ref, xn_ref, cw_ref, cb_ref, sp_ref, wg_ref, ba_ref, bx_ref, h0_ref,
                y_ref, ht_ref, xpad_ref, xc_ref, a_ref, b_ref, st_ref, *, cb, r, nb):
    d = pl.program_id(0)
    j = pl.program_id(1)
    nj = pl.num_programs(1)
    jeff = j + d * (nj - 1 - 2 * j)
    steps = cb * r
    rows = steps * nb
    w = x_ref.shape[-1]

    @pl.when(j == 0)
    def _():
        st_ref[...] = h0_ref[0]

    has_prev = (jeff > 0).astype(F32)
    has_next = (jeff < nj - 1).astype(F32)
    xpad_ref[0:nb, :] = xp_ref[0, 0].astype(F32) * has_prev
    xpad_ref[nb:nb + rows, :] = x_ref[...].reshape(rows, w).astype(F32)
    xpad_ref[nb + rows:3 * nb + rows, :] = xn_ref[0].reshape(2 * nb, w).astype(F32) * has_next
    acc = cb_ref[...] + cw_ref[0:1, :] * xpad_ref[0:rows, :]
    for k in range(1, CONV_W):
        acc = acc + cw_ref[k:k + 1, :] * xpad_ref[k * nb:k * nb + rows, :]
    xc_ref[...] = acc

    for s in range(w // LANES):
        ls = slice(s * LANES, (s + 1) * LANES)
        xs = xc_ref[:, ls]
        gs = jnp.dot(xs.astype(BF16), wg_ref[0, s], preferred_element_type=F32)
        rg = _sigmoid(gs[:, :LANES] + ba_ref[0, :, ls])
        ig = _sigmoid(gs[:, LANES:] + bx_ref[0, :, ls])
        log_a = (-LRU_C) * rg * sp_ref[0, :, ls]
        a = jnp.exp(log_a)
        a_ref[:, ls] = a
        b_ref[:, ls] = jnp.sqrt(1.0 - a * a) * (ig * xs)

    def step(i, h):
        t = jnp.where(d == 0, i, steps - 1 - i)
        r0 = pl.multiple_of(t * nb, nb)
        h = a_ref[pl.ds(r0, nb), :] * h + b_ref[pl.ds(r0, nb), :]
        y_ref[0, t // r, t % r] = h
        return h

    st_ref[...] = lax.fori_loop(0, steps, step, st_ref[...])

    @pl.when(j == nj - 1)
    def _():
        ht_ref[0] = st_ref[...]


def _lru_scan(x4, conv_w, conv_b, sp, wg, b_a, b_x, h0, *, cb):
    c, r, nb, w = x4.shape
    nj = c // cb
    rows = cb * r * nb
    blk = lambda d, j: j + d * (nj - 1 - 2 * j)
    full2 = lambda a: pl.BlockSpec(a.shape, lambda d, j: (0,) * a.ndim)
    perdir = lambda a: pl.BlockSpec((1,) + a.shape[1:], lambda d, j: (d,) + (0,) * (a.ndim - 1))
    return pl.pallas_call(
        functools.partial(_lru_kernel, cb=cb, r=r, nb=nb),
        grid=(2, nj),
        in_specs=[pl.BlockSpec((cb, r, nb, w), lambda d, j: (blk(d, j), 0, 0, 0)),
                  pl.BlockSpec((1, 1, nb, w),
                               lambda d, j: (jnp.maximum(blk(d, j) * cb - 1, 0), r - 1, 0, 0)),
                  pl.BlockSpec((1, 2, nb, w),
                               lambda d, j: (jnp.minimum((blk(d, j) + 1) * cb, c - 1), 0, 0, 0)),
                  full2(conv_w), full2(conv_b), perdir(sp), perdir(wg), perdir(b_a), perdir(b_x),
                  perdir(h0)],
        out_specs=[pl.BlockSpec((1, cb, r, nb, w), lambda d, j: (d, blk(d, j), 0, 0, 0)),
                   pl.BlockSpec((1, nb, w), lambda d, j: (d, 0, 0))],
        out_shape=[jax.ShapeDtypeStruct((2, c, r, nb, w), F32),
                   jax.ShapeDtypeStruct((2, nb, w), F32)],
        scratch_shapes=[pltpu.VMEM((rows + 3 * nb, w), F32),
                        pltpu.VMEM((rows, w), F32),
                        pltpu.VMEM((rows, w), F32),
                        pltpu.VMEM((rows, w), F32),
                        pltpu.VMEM((nb, w), F32)],
        compiler_params=_cparams("arbitrary", "arbitrary"),
        name="lru_scan",
    )(x4, x4, x4, conv_w, conv_b, sp, wg, b_a, b_x, h0)


def _lru_gate_weights(w_a, w_x):
    two, nblk, k, _ = w_a.shape
    per = LANES // k
    eye = jnp.eye(per, dtype=F32)

    def slabs(wm):
        wm = wm.astype(F32).reshape(two, nblk // per, per, k, k)
        m = jnp.einsum('dsgij,gk->dsgikj', wm, eye)
        return m.reshape(two, nblk // per, LANES, LANES)

    return jnp.concatenate([slabs(w_a), slabs(w_x)], axis=-1).astype(BF16)


def _k2_kernel(x_ref, u_ref, ys_ref, yl_ref, gg_ref, m5_ref, ml_ref, d5_ref, wglu_ref, bglu_ref,
               wp5_ref, wpl_ref, wo_ref, g1_ref, gf_ref, sh_ref, sc_ref, wr_ref, br_ref,
               x1_ref, hn_ref, ti_ref, tw_ref, cnt_ref, *, nb):
    tm, dm = x_ref.shape
    u = u_ref[...]
    y5 = ys_ref[0] + ys_ref[1] + d5_ref[...] * u
    gy = _gelu(y5)
    o5 = gy * _sigmoid(jnp.dot(gy.astype(BF16), wglu_ref[...], preferred_element_type=F32)
                       + bglu_ref[...])
    yl = (yl_ref[0] + yl_ref[1]).reshape(tm, -1)
    ol = _gelu(gg_ref[...].astype(F32)) * yl
    merged = (_sigmoid(m5_ref[...].astype(F32))
              * jnp.dot(o5.astype(BF16), wp5_ref[...], preferred_element_type=F32)
              + _sigmoid(ml_ref[...].astype(F32))
              * jnp.dot(ol.astype(BF16), wpl_ref[...], preferred_element_type=F32))
    mix = jnp.dot(merged.astype(BF16), wo_ref[...], preferred_element_type=F32)
    x1 = (x_ref[...].reshape(tm // nb, nb, dm) + g1_ref[...][None] * mix.reshape(tm // nb, nb, dm))
    x1 = x1.reshape(tm, dm)
    x1_ref[...] = x1
    hn = _rms_mod(x1, gf_ref[...], sh_ref[...], sc_ref[...], nb)
    hn_ref[...] = hn.astype(BF16)

    logits = lax.dot_general(wr_ref[...], hn, (((1,), (1,)), ((), ())),
                             preferred_element_type=F32, precision=HIGHEST) + br_ref[...]
    ne = logits.shape[0]
    eidx = lax.broadcasted_iota(I32, (ne, tm), 0).astype(F32)
    vals = logits
    tv, ti = [], []
    for _ in range(TOP_K):
        m = jnp.max(vals, axis=0, keepdims=True)
        idx = jnp.min(jnp.where(vals == m, eidx, float(ne)), axis=0, keepdims=True)
        tv.append(m)
        ti.append(idx)
        vals = jnp.where(eidx == idx, -jnp.inf, vals)
    ex = [jnp.exp(v - tv[0]) for v in tv]
    den = ex[0] + ex[1] + ex[2] + ex[3]
    ti_ref[...] = jnp.concatenate(ti, axis=0).astype(I32)
    tw_ref[...] = jnp.concatenate([e / den for e in ex], axis=0)
    sel = (eidx == ti[0]) | (eidx == ti[1]) | (eidx == ti[2]) | (eidx == ti[3])
    self32 = jnp.where(sel, 1.0, 0.0)
    for sb in range(tm // SUB):
        cnt_ref[sb] = jnp.sum(self32[:, sb * SUB:(sb + 1) * SUB], axis=1, keepdims=True).astype(I32)


def _k2(x2, u, ys, yl5, gg, m5, ml, d5, wglu, bglu, wp5, wpl, wo, g1, gf, sh, sc, wr_t, br,
        *, nb, r, colmajor):
    n, dm = x2.shape
    t = n // nb
    ncol = t // r
    lw = yl5.shape[-1]
    s5w = u.shape[-1]
    ne = wr_t.shape[0]
    if colmajor:
        cbk = max(1, min(ncol, K2_TILE // nb))
        per_row = ncol // cbk
        tm = cbk * nb
        yl_spec = pl.BlockSpec((2, cbk, 1, nb, lw),
                               lambda i: (0, i % per_row, i // per_row, 0, 0))
    else:
        cbk = max(1, min(ncol, K2_TILE // (r * nb)))
        tm = cbk * r * nb
        yl_spec = pl.BlockSpec((2, cbk, r, nb, lw), lambda i: (0, i, 0, 0, 0))
    row = lambda w: pl.BlockSpec((tm, w), lambda i: (i, 0))
    full = lambda a: pl.BlockSpec(a.shape, lambda i: (0,) * a.ndim)
    ys2 = ys.reshape(2, n, s5w)
    params = (d5, wglu, bglu, wp5, wpl, wo, g1, gf, sh, sc, wr_t, br)
    return pl.pallas_call(
        functools.partial(_k2_kernel, nb=nb),
        grid=(n // tm,),
        in_specs=[row(dm), row(s5w), pl.BlockSpec((2, tm, s5w), lambda i: (0, i, 0)), yl_spec,
                  row(lw), row(dm), row(dm)] + [full(p) for p in params],
        out_specs=[row(dm), row(dm),
                   pl.BlockSpec((TOP_K, tm), lambda i: (0, i)),
                   pl.BlockSpec((TOP_K, tm), lambda i: (0, i)),
                   pl.BlockSpec((tm // SUB, ne, 1), lambda i: (i, 0, 0))],
        out_shape=[jax.ShapeDtypeStruct((n, dm), F32),
                   jax.ShapeDtypeStruct((n, dm), BF16),
                   jax.ShapeDtypeStruct((TOP_K, n), I32),
                   jax.ShapeDtypeStruct((TOP_K, n), F32),
                   jax.ShapeDtypeStruct((n // SUB, ne, 1), I32)],
        compiler_params=_cparams("arbitrary"),
        name="k2_mix_out",
    )(x2, u, ys2, yl5, gg, m5, ml, *params)


def _seg_copies(n_seg, pcs, src_off, dst_off, src_ref, dst_ref, sem, base):
    def per_seg(e, _):
        n = pcs[base + e] // SEG_PAD
        so = src_off[base + e]
        do = dst_off[base + e]

        def per_chunk(c, _):
            s0 = pl.multiple_of(so + c * SEG_PAD, SEG_PAD)
            d0 = pl.multiple_of(do + c * SEG_PAD, SEG_PAD)
            pltpu.make_async_copy(src_ref.at[pl.ds(s0, SEG_PAD)], dst_ref.at[pl.ds(d0, SEG_PAD)],
                                  sem).start()
            return 0

        lax.fori_loop(0, n, per_chunk, 0)
        return 0

    lax.fori_loop(0, n_seg, per_seg, 0)


def _wait_chunks(n, src_ref, dst_ref, sem):
    def one(c, _):
        pltpu.make_async_copy(src_ref.at[pl.ds(0, SEG_PAD)], dst_ref.at[pl.ds(0, SEG_PAD)],
                              sem).wait()
        return 0

    lax.fori_loop(0, n, one, 0)


def _dispatch_kernel(pc_ref, loc_ref, glob_ref, nch_ref, tail0_ref, tailn_ref,
                     hn_ref, ti_ref, locv_ref, xs_ref, slot_ref, xg_ref, zero_ref, sem, tsem,
                     *, ne):
    s = pl.program_id(0)
    ns = pl.num_programs(0)
    buf = s % 2

    @pl.when(s == 0)
    def _():
        zero_ref[...] = jnp.zeros_like(zero_ref)

        def per_e(e, _):
            def per_chunk(c, _):
                d0 = pl.multiple_of(tail0_ref[e] + c * SEG_PAD, SEG_PAD)
                pltpu.make_async_copy(zero_ref, xs_ref.at[pl.ds(d0, SEG_PAD)], tsem).start()
                return 0
            lax.fori_loop(0, tailn_ref[e], per_chunk, 0)
            return 0
        lax.fori_loop(0, ne, per_e, 0)

        def per_e_wait(e, _):
            def per_chunk(c, _):
                pltpu.make_async_copy(zero_ref, xs_ref.at[pl.ds(0, SEG_PAD)], tsem).wait()
                return 0
            lax.fori_loop(0, tailn_ref[e], per_chunk, 0)
            return 0
        lax.fori_loop(0, ne, per_e_wait, 0)

    ti = ti_ref[...].astype(F32)
    eidx = lax.broadcasted_iota(I32, (ne, SUB), 0).astype(F32)
    hit = [eidx == ti[k:k + 1] for k in range(TOP_K)]
    sel = hit[0] | hit[1] | hit[2] | hit[3]
    upper = (lax.broadcasted_iota(I32, (SUB, SUB), 0)
             < lax.broadcasted_iota(I32, (SUB, SUB), 1))
    rank = jnp.dot(jnp.where(sel, 1.0, 0.0).astype(BF16), jnp.where(upper, 1.0, 0.0).astype(BF16),
                   preferred_element_type=F32)
    slot_e = locv_ref[0].astype(F32) + rank
    slots = [jnp.sum(jnp.where(hit[k], slot_e, 0.0), axis=0, keepdims=True) for k in range(TOP_K)]
    slot_ref[...] = jnp.concatenate(slots, axis=0).astype(I32)

    hn = hn_ref[...]
    blk = 256
    for rb in range(SLOT_ROWS // blk):
        sidx = (lax.broadcasted_iota(I32, (blk, SUB), 0) + rb * blk).astype(F32)
        p = (sidx == slots[0]) | (sidx == slots[1]) | (sidx == slots[2]) | (sidx == slots[3])
        xg = jnp.dot(jnp.where(p, 1.0, 0.0).astype(BF16), hn, preferred_element_type=F32)
        xg_ref[buf, rb * blk:(rb + 1) * blk, :] = xg.astype(BF16)

    _seg_copies(ne, pc_ref, loc_ref, glob_ref, xg_ref.at[buf], xs_ref, sem.at[buf], s * ne)

    @pl.when(s > 0)
    def _():
        _wait_chunks(nch_ref[s - 1], xg_ref.at[1 - buf], xs_ref, sem.at[1 - buf])

    @pl.when(s == ns - 1)
    def _():
        _wait_chunks(nch_ref[s], xg_ref.at[buf], xs_ref, sem.at[buf])


def _dispatch(hn, ti, meta, *, ne):
    n, dm = hn.shape
    ns = n // SUB
    gs = pltpu.PrefetchScalarGridSpec(
        num_scalar_prefetch=6,
        grid=(ns,),
        in_specs=[pl.BlockSpec((SUB, dm), lambda s, *_: (s, 0)),
                  pl.BlockSpec((TOP_K, SUB), lambda s, *_: (0, s)),
                  pl.BlockSpec((1, ne, 1), lambda s, *_: (s, 0, 0))],
        out_specs=[pl.BlockSpec(memory_space=pl.ANY),
                   pl.BlockSpec((TOP_K, SUB), lambda s, *_: (0, s))],
        scratch_shapes=[pltpu.VMEM((2, SLOT_ROWS, dm), BF16),
                        pltpu.VMEM((SEG_PAD, dm), BF16),
                        pltpu.SemaphoreType.DMA((2,)),
                        pltpu.SemaphoreType.DMA(())],
    )
    return pl.pallas_call(
        functools.partial(_dispatch_kernel, ne=ne),
        grid_spec=gs,
        out_shape=[jax.ShapeDtypeStruct((meta['p_max'], dm), BF16),
                   jax.ShapeDtypeStruct((TOP_K, n), I32)],
        compiler_params=_cparams("arbitrary", disable_bounds_checks=True),
        name="moe_dispatch",
    )(meta['pc'], meta['loc'], meta['glob'], meta['nch'], meta['tail0'], meta['tailn'],
      hn, ti, meta['locv'])


def _expert_kernel(te_ref, nt_ref, x_ref, wu_ref, b1_ref, wd_ref, bd_ref, y_ref,
                   w1_ref, w2_ref, act_ref):
    i = pl.program_id(0)
    live = i < nt_ref[0]
    new_expert = (i == 0) | (te_ref[jnp.maximum(i - 1, 0)] != te_ref[i])
    pair = 2 * LANES
    nblk = wu_ref.shape[-1] // pair

    @pl.when(live & new_expert)
    def _():
        rr = lax.broadcasted_iota(I32, (pair, pair), 0)
        cc = lax.broadcasted_iota(I32, (pair, pair), 1)
        src = jnp.where(cc < LANES, 2 * cc, 2 * (cc - LANES) + 1)
        perm = jnp.where(rr == src, 1.0, 0.0).astype(BF16)
        for jb in range(nblk):
            cols = slice(jb * pair, (jb + 1) * pair)
            w1_ref[:, cols] = jnp.dot(wu_ref[0, :, cols].astype(BF16), perm,
                                      preferred_element_type=F32).astype(BF16)
        w2_ref[...] = wd_ref[0].astype(BF16)

    @pl.when(live)
    def _():
        x = x_ref[...]
        for jb in range(nblk):
            cols = slice(jb * pair, (jb + 1) * pair)
            hb = jnp.dot(x, w1_ref[:, cols], preferred_element_type=F32) + b1_ref[0, :, cols]
            gate = jnp.minimum(hb[:, :LANES], SWIGLU_LIMIT)
            up = jnp.clip(hb[:, LANES:], -SWIGLU_LIMIT, SWIGLU_LIMIT)
            act = gate * _sigmoid(SWIGLU_ALPHA * gate) * (up + 1.0)
            act_ref[:, jb * LANES:(jb + 1) * LANES] = act.astype(BF16)
        y = jnp.dot(act_ref[...], w2_ref[...], preferred_element_type=F32) + bd_ref[0]
        y_ref[...] = y.astype(y_ref.dtype)


def _experts(xs, w_up, b1p, w_down, b_down, meta):
    p_max, dm = xs.shape
    ne, _, f2 = w_up.shape
    f = f2 // 2
    n_tiles = p_max // EXPERT_TILE
    tile = lambda i, te, nt: (jnp.minimum(i, nt[0] - 1), 0)
    byexp = lambda i, te, nt: (te[i], 0, 0)
    gs = pltpu.PrefetchScalarGridSpec(
        num_scalar_prefetch=2,
        grid=(n_tiles,),
        in_specs=[pl.BlockSpec((EXPERT_TILE, dm), tile),
                  pl.BlockSpec((1, dm, f2), byexp), pl.BlockSpec((1, 1, f2), byexp),
                  pl.BlockSpec((1, f, dm), byexp), pl.BlockSpec((1, 1, dm), byexp)],
        out_specs=pl.BlockSpec((EXPERT_TILE, dm), tile),
        scratch_shapes=[pltpu.VMEM((dm, f2), BF16), pltpu.VMEM((f, dm), BF16),
                        pltpu.VMEM((EXPERT_TILE, f), BF16)],
    )
    return pl.pallas_call(
        _expert_kernel,
        grid_spec=gs,
        out_shape=jax.ShapeDtypeStruct((p_max, dm), BF16),
        compiler_params=_cparams("arbitrary"),
        name="moe_experts",
    )(meta['tile_e'], meta['n_tiles'], xs, w_up, b1p, w_down, b_down)


def _combine_kernel(pc_ref, loc_ref, glob_ref, nch_ref, x_ref, slot_ref, tw_ref, g2_ref, gfin_ref,
                    ys_ref, o_ref, yl_ref, sem, *, ne, s0, nb, final_norm):
    i = pl.program_id(0)
    ni = pl.num_programs(0)
    buf = i % 2

    def fetch(step, b):
        _seg_copies(ne, pc_ref, glob_ref, loc_ref, ys_ref, yl_ref.at[b], sem.at[b],
                    (s0 + step) * ne)

    @pl.when(i == 0)
    def _():
        yl_ref[...] = jnp.zeros_like(yl_ref)
        fetch(0, 0)

    @pl.when(i + 1 < ni)
    def _():
        fetch(i + 1, 1 - buf)

    _wait_chunks(nch_ref[s0 + i], ys_ref, yl_ref.at[buf], sem.at[buf])

    slot = slot_ref[...].astype(F32)
    tw = tw_ref[...]
    sidx = lax.broadcasted_iota(I32, (SUB, SLOT_ROWS), 1).astype(F32)
    pw = jnp.where(sidx == slot[:, 0:1], tw[:, 0:1], 0.0)
    for k in range(1, TOP_K):
        pw = pw + jnp.where(sidx == slot[:, k:k + 1], tw[:, k:k + 1], 0.0)
    f = jnp.dot(pw.astype(BF16), yl_ref[buf], preferred_element_type=F32)
    dm = f.shape[-1]
    x2 = x_ref[...].reshape(SUB // nb, nb, dm) + g2_ref[...][None] * f.reshape(SUB // nb, nb, dm)
    x2 = x2.reshape(SUB, dm)
    if final_norm:
        x2 = x2 * lax.rsqrt(jnp.mean(x2 * x2, axis=-1, keepdims=True) + EPS) * gfin_ref[...]
    o_ref[...] = x2


def _combine(x2d, slot_t, tw_t, g2, gfin, ys, meta, *, ne, s0, nb, final_norm):
    n, dm = x2d.shape
    ns = n // SUB
    gs = pltpu.PrefetchScalarGridSpec(
        num_scalar_prefetch=4,
        grid=(ns,),
        in_specs=[pl.BlockSpec((SUB, dm), lambda i, *_: (i, 0)),
                  pl.BlockSpec((SUB, TOP_K), lambda i, *_: (s0 + i, 0)),
                  pl.BlockSpec((SUB, TOP_K), lambda i, *_: (s0 + i, 0)),
                  pl.BlockSpec(g2.shape, lambda i, *_: (0, 0)),
                  pl.BlockSpec(gfin.shape, lambda i, *_: (0, 0)),
                  pl.BlockSpec(memory_space=pl.ANY)],
        out_specs=pl.BlockSpec((SUB, dm), lambda i, *_: (i, 0)),
        scratch_shapes=[pltpu.VMEM((2, SLOT_ROWS, dm), BF16),
                        pltpu.SemaphoreType.DMA((2,))],
    )
    return pl.pallas_call(
        functools.partial(_combine_kernel, ne=ne, s0=s0, nb=nb, final_norm=final_norm),
        grid_spec=gs,
        out_shape=jax.ShapeDtypeStruct((n, dm), F32),
        compiler_params=_cparams("arbitrary", disable_bounds_checks=True),
        name="moe_combine",
    )(meta['pc'], meta['loc'], meta['glob'], meta['nch'], x2d, slot_t, tw_t, g2, gfin, ys)


def _routing_meta(cnt, ne):
    ns = cnt.shape[0]
    pc = (cnt + (SEG_PAD - 1)) // SEG_PAD * SEG_PAD
    loc = jnp.cumsum(pc, axis=1) - pc
    tot = jnp.sum(pc, axis=0)
    reg = (tot + (EXPERT_TILE - 1)) // EXPERT_TILE * EXPERT_TILE
    reg_end = jnp.cumsum(reg)
    base = reg_end - reg
    glob = base[None, :] + jnp.cumsum(pc, axis=0) - pc
    p_max = ns * (SUB * TOP_K + ne * (SEG_PAD - 1)) + ne * (EXPERT_TILE - SEG_PAD)
    p_max = (p_max + EXPERT_TILE - 1) // EXPERT_TILE * EXPERT_TILE
    n_tiles_max = p_max // EXPERT_TILE
    tile_start = jnp.arange(n_tiles_max, dtype=I32) * EXPERT_TILE
    tile_e = jnp.sum((tile_start[:, None] >= reg_end[None, :]).astype(I32), axis=1)
    tile_e = jnp.minimum(tile_e, ne - 1)
    return dict(pc=pc.reshape(-1).astype(I32), loc=loc.reshape(-1).astype(I32),
                glob=glob.reshape(-1).astype(I32),
                nch=(jnp.sum(pc, axis=1) // SEG_PAD).astype(I32),
                tail0=(base + tot).astype(I32), tailn=((reg - tot) // SEG_PAD).astype(I32),
                locv=loc.reshape(ns, ne, 1).astype(I32),
                tile_e=tile_e, n_tiles=(reg_end[-1:] // EXPERT_TILE).astype(I32), p_max=p_max)


def kernel(x, c, ctx, c_ctx, w_ada, b_ada, g_mix, g_ffn, w_in, s5_lam_re, s5_lam_im, s5_log_dt, s5_b_re, s5_b_im, s5_c_re, s5_c_im, s5_d, s5_w_glu, s5_b_glu, lru_conv_w, lru_conv_b, lru_lam, lru_w_a, lru_b_a, lru_w_x, lru_b_x, w_proj_s5, w_proj_lru, w_out, w_router, b_router, w_up, b_up, w_down, b_down, g_final):
    nb, seq, dm = x.shape
    tc = ctx.shape[1]
    depth = w_ada.shape[0]
    r = seq // GRID_W
    s5w = s5_d.shape[-1]
    lw = lru_conv_w.shape[-1]
    ne = w_router.shape[-1]
    nl, ncx = seq * nb, tc * nb

    xl = jnp.transpose(x, (1, 0, 2)).reshape(nl, dm)
    xc = jnp.transpose(ctx, (1, 0, 2)).reshape(ncx, dm)

    pad_rows = (-(nb + 1)) % 8
    cvec = jnp.concatenate([c, c_ctx[None, :], jnp.zeros((pad_rows, dm), F32)], axis=0)
    ada = _ada_all(cvec, w_ada, b_ada)

    for l in range(depth):
        need_ctx = l < depth - 1
        mods_l = [ada[l, :nb, k * dm:(k + 1) * dm] for k in range(6)]
        mods_c = [jnp.broadcast_to(ada[l, nb:nb + 1, k * dm:(k + 1) * dm], (nb, dm)) for k in range(6)]
        gm = g_mix[l][None, :]
        gf = g_ffn[l][None, :]
        w_in_l = w_in[l].astype(BF16)

        u_l, xr_l, gg_l, m5_l, ml_l = _k1(xl, gm, mods_l[0], mods_l[1], w_in_l, nb=nb, r=r,
                                          colmajor=True, s5w=s5w, lw=lw, need_gates=True)
        outs_c = _k1(xc, gm, mods_c[0], mods_c[1], w_in_l, nb=nb, r=r, colmajor=False,
                     s5w=s5w, lw=lw, need_gates=need_ctx)
        u_c, xr_c = outs_c[0], outs_c[1]

        bm, cm, a_re, a_im = _s5_operators(s5_lam_re[l], s5_lam_im[l], s5_log_dt[l], s5_b_re[l],
                                           s5_b_im[l], s5_c_re[l], s5_c_im[l])
        nslab = s5w // LANES
        h0 = jnp.zeros((2, nslab, nb, 2 * S5_SLAB_STATE), F32)
        ys_c, hs_c = _s5_scan(u_c.reshape(tc, nb, s5w), bm, cm, a_re, a_im, h0)
        ys_l, _ = _s5_scan(u_l.reshape(seq, nb, s5w), bm, cm, a_re, a_im, hs_c)

        lam = lru_lam[l].astype(F32)
        sp = (jnp.maximum(-lam, 0.0) + jnp.log1p(jnp.exp(-jnp.abs(lam))))[:, None, :]
        wg = _lru_gate_weights(lru_w_a[l], lru_w_x[l])
        cw = lru_conv_w[l]
        cbias = lru_conv_b[l][None, :]
        b_a = lru_b_a[l][:, None, :]
        b_x = lru_b_x[l][:, None, :]
        hz = jnp.zeros((2, nb, lw), F32)
        cb_c = 2 if (tc // r) % 2 == 0 else 1
        cb_l = 2 if GRID_W % 2 == 0 else 1
        yl_c, hl_c = _lru_scan(xr_c, cw, cbias, sp, wg, b_a, b_x, hz, cb=cb_c)
        yl_l, _ = _lru_scan(xr_l, cw, cbias, sp, wg, b_a, b_x, hl_c, cb=cb_l)

        k2_params = (s5_d[l][None, :], s5_w_glu[l].astype(BF16), s5_b_glu[l][None, :],
                     w_proj_s5[l].astype(BF16), w_proj_lru[l].astype(BF16), w_out[l].astype(BF16))
        wr_t = jnp.transpose(w_router[l])
        br = b_router[l][:, None]
        x1_l, hn_l, ti_l, tw_l, cnt_l = _k2(xl, u_l, ys_l, yl_l, gg_l, m5_l, ml_l, *k2_params,
                                            mods_l[2], gf, mods_l[3], mods_l[4], wr_t, br,
                                            nb=nb, r=r, colmajor=True)
        if need_ctx:
            _, _, gg_c, m5_c, ml_c = outs_c
            x1_c, hn_c, ti_c, tw_c, cnt_c = _k2(xc, u_c, ys_c, yl_c, gg_c, m5_c, ml_c, *k2_params,
                                                mods_c[2], gf, mods_c[3], mods_c[4], wr_t, br,
                                                nb=nb, r=r, colmajor=False)
            hn = jnp.concatenate([hn_c, hn_l], axis=0)
            ti = jnp.concatenate([ti_c, ti_l], axis=1)
            tw = jnp.concatenate([tw_c, tw_l], axis=1)
            cnt = jnp.concatenate([cnt_c, cnt_l], axis=0)
        else:
            hn, ti, tw, cnt = hn_l, ti_l, tw_l, cnt_l

        meta = _routing_meta(cnt[:, :, 0], ne)
        xs, slots = _dispatch(hn, ti, meta, ne=ne)
        f2 = w_up.shape[-1]
        b1p = jnp.transpose(b_up[l].reshape(ne, f2 // (2 * LANES), LANES, 2), (0, 1, 3, 2))
        ys = _experts(xs, w_up[l], b1p.reshape(ne, 1, f2), w_down[l], b_down[l][:, None, :], meta)
        slot_t = jnp.transpose(slots)
        tw_t = jnp.transpose(tw)
        last = l == depth - 1
        gfin = g_final[None, :]
        if need_ctx:
            xc = _combine(x1_c, slot_t, tw_t, mods_c[5], gfin, ys, meta, ne=ne, s0=0, nb=nb,
                          final_norm=False)
            xl = _combine(x1_l, slot_t, tw_t, mods_l[5], gfin, ys, meta, ne=ne, s0=ncx // SUB,
                          nb=nb, final_norm=False)
        else:
            xl = _combine(x1_l, slot_t, tw_t, mods_l[5], gfin, ys, meta, ne=ne, s0=0, nb=nb,
                          final_norm=last)

    return jnp.transpose(xl.reshape(seq, nb, dm), (1, 0, 2)).astype(x.dtype)
```

```python
import functools
import math

import jax
import jax.numpy as jnp
from jax import lax
from jax.experimental import pallas as pl
from jax.experimental.pallas import tpu as pltpu

F32 = jnp.float32
BF16 = jnp.bfloat16
I32 = jnp.int32
HIGHEST = lax.Precision.HIGHEST

GRID_W = 64
S5_GROUP = 16
S5_STATE = 64
S5_DT_MIN = 0.001
S5_DT_MAX = 0.1
LRU_BLOCK = 64
LRU_C = 8.0
CONV_W = 4
TOP_K = 4
SWIGLU_LIMIT = 7.0
SWIGLU_ALPHA = 1.702
EPS = 1e-6

LANES = 128
BF16_ROWS = 16
VMEM_LIMIT = 56 * 1024 * 1024

ROW_TILE = 1024
K2_TILE = 512
S5_SLAB_GROUPS = LANES // S5_GROUP
S5_SLAB_STATE = S5_SLAB_GROUPS * S5_STATE
SUB = 256
SEG_PAD = BF16_ROWS
SLOT_ROWS = 1536
MAX_CHUNKS = SLOT_ROWS // SEG_PAD
EXPERT_TILE = 512


def _cparams(*sem, **kw):
    return pltpu.CompilerParams(dimension_semantics=sem, vmem_limit_bytes=VMEM_LIMIT, **kw)


def _gelu(x):
    return 0.5 * x * (1.0 + jnp.tanh(math.sqrt(2.0 / math.pi) * (x + 0.044715 * (x * x * x))))


def _sigmoid(x):
    return 0.5 * jnp.tanh(0.5 * x) + 0.5


def _rms_mod(x, g, sh, sc, nb):
    tm, d = x.shape
    y = x * lax.rsqrt(jnp.mean(x * x, axis=-1, keepdims=True) + EPS) * g
    y3 = y.reshape(tm // nb, nb, d) * (1.0 + sc)[None] + sh[None]
    return y3.reshape(tm, d)


def _ada_kernel(c_ref, w_ref, b_ref, o_ref):
    c = c_ref[...]
    cond = c * _sigmoid(c)
    o_ref[0] = jnp.dot(cond, w_ref[0], preferred_element_type=F32, precision=HIGHEST) + b_ref[0]


def _ada_all(cvec, w_ada, b_ada):
    nl, d, w6 = w_ada.shape
    r = cvec.shape[0]
    tn = min(w6, 1536)
    return pl.pallas_call(
        _ada_kernel,
        grid=(nl, w6 // tn),
        in_specs=[pl.BlockSpec((r, d), lambda l, j: (0, 0)),
                  pl.BlockSpec((1, d, tn), lambda l, j: (l, 0, j)),
                  pl.BlockSpec((1, 1, tn), lambda l, j: (l, 0, j))],
        out_specs=pl.BlockSpec((1, r, tn), lambda l, j: (l, 0, j)),
        out_shape=jax.ShapeDtypeStruct((nl, r, w6), F32),
        compiler_params=_cparams("arbitrary", "arbitrary"),
        name="ada",
    )(cvec, w_ada, b_ada.reshape(nl, 1, w6))


def _k1(x2, g, sh, sc, w_in, *, nb, r, colmajor, s5w, lw, need_gates):
    n, d = x2.shape
    t = n // nb
    ncol = t // r
    if colmajor:
        tm = (t // r) * nb
        xr_spec = pl.BlockSpec((ncol, 1, nb, lw), lambda i: (0, i, 0, 0))
    else:
        cbk = max(1, min(ncol, ROW_TILE // (r * nb)))
        tm = cbk * r * nb
        xr_spec = pl.BlockSpec((cbk, r, nb, lw), lambda i: (i, 0, 0, 0))
    splits = (s5w, s5w + lw, s5w + 2 * lw, s5w + 2 * lw + d, s5w + 2 * lw + 2 * d)
    if not need_gates:
        splits = splits[:2]
    row = lambda w: pl.BlockSpec((tm, w), lambda i: (i, 0))
    full = lambda a: pl.BlockSpec(a.shape, lambda i: (0,) * a.ndim)
    out_specs = [row(s5w), xr_spec, row(lw), row(d), row(d)]
    out_shape = [jax.ShapeDtypeStruct((n, s5w), F32),
                 jax.ShapeDtypeStruct((ncol, r, nb, lw), F32),
                 jax.ShapeDtypeStruct((n, lw), BF16),
                 jax.ShapeDtypeStruct((n, d), BF16),
                 jax.ShapeDtypeStruct((n, d), BF16)]
    k = len(splits)

    def body(x_ref, g_ref, sh_ref, sc_ref, w_ref, *o_refs):
        hn = _rms_mod(x_ref[...], g_ref[...], sh_ref[...], sc_ref[...], nb).astype(BF16)
        lo = 0
        for o_ref, hi in zip(o_refs, splits):
            v = jnp.dot(hn, w_ref[:, lo:hi], preferred_element_type=F32).astype(o_ref.dtype)
            o_ref[...] = v.reshape(o_ref.shape)
            lo = hi

    w_used = w_in[:, :splits[-1]]
    return pl.pallas_call(
        body,
        grid=(n // tm,),
        in_specs=[row(d), full(g), full(sh), full(sc), full(w_used)],
        out_specs=out_specs[:k],
        out_shape=out_shape[:k],
        compiler_params=_cparams("arbitrary"),
        name="k1_in_proj",
    )(x2, g, sh, sc, w_used)


def _s5_kernel(u_ref, bm_ref, cm_ref, are_ref, aim_ref, h0_ref, y_ref, ht_ref, hs_ref, st_ref,
               *, tt, nb, nslab):
    d = pl.program_id(0)
    j = pl.program_id(1)
    nj = pl.num_programs(1)
    sw = S5_SLAB_STATE
    rows = tt * nb

    @pl.when(j == 0)
    def _():
        st_ref[...] = h0_ref[0]

    def run(reverse):
        for m in range(nslab):
            ub = u_ref[:, :, m * LANES:(m + 1) * LANES].reshape(rows, LANES).astype(BF16)
            hs_ref[m] = jnp.dot(ub, bm_ref[0, m], preferred_element_type=F32)
            a_re = jnp.broadcast_to(are_ref[0, m], (nb, sw))
            a_im = jnp.broadcast_to(aim_ref[0, m], (nb, sw))
            hr = st_ref[m, :, :sw]
            hi = st_ref[m, :, sw:]
            for i in range(tt):
                t = tt - 1 - i if reverse else i
                rs = slice(t * nb, (t + 1) * nb)
                nr = a_re * hr - a_im * hi + hs_ref[m, rs, :sw]
                ni = a_re * hi + a_im * hr + hs_ref[m, rs, sw:]
                hs_ref[m, rs, :sw] = nr
                hs_ref[m, rs, sw:] = ni
                hr, hi = nr, ni
            st_ref[m, :, :sw] = hr
            st_ref[m, :, sw:] = hi
            y = jnp.dot(hs_ref[m].astype(BF16), cm_ref[0, m], preferred_element_type=F32)
            y_ref[0, :, :, m * LANES:(m + 1) * LANES] = y.reshape(tt, nb, LANES)

    @pl.when(d == 0)
    def _():
        run(False)

    @pl.when(d == 1)
    def _():
        run(True)

    @pl.when(j == nj - 1)
    def _():
        ht_ref[0] = st_ref[...]


def _s5_scan(u3, bm, cm, a_re, a_im, h0):
    t, nb, s5w = u3.shape
    nslab = s5w // LANES
    tt = 64 if t % 64 == 0 else t
    nj = t // tt
    sw2 = 2 * S5_SLAB_STATE
    blk = lambda d, j: j + d * (nj - 1 - 2 * j)
    return pl.pallas_call(
        functools.partial(_s5_kernel, tt=tt, nb=nb, nslab=nslab),
        grid=(2, nj),
        in_specs=[pl.BlockSpec((tt, nb, s5w), lambda d, j: (blk(d, j), 0, 0)),
                  pl.BlockSpec((1, nslab, LANES, sw2), lambda d, j: (d, 0, 0, 0)),
                  pl.BlockSpec((1, nslab, sw2, LANES), lambda d, j: (d, 0, 0, 0)),
                  pl.BlockSpec((1, nslab, 1, S5_SLAB_STATE), lambda d, j: (d, 0, 0, 0)),
                  pl.BlockSpec((1, nslab, 1, S5_SLAB_STATE), lambda d, j: (d, 0, 0, 0)),
                  pl.BlockSpec((1, nslab, nb, sw2), lambda d, j: (d, 0, 0, 0))],
        out_specs=[pl.BlockSpec((1, tt, nb, s5w), lambda d, j: (d, blk(d, j), 0, 0)),
                   pl.BlockSpec((1, nslab, nb, sw2), lambda d, j: (d, 0, 0, 0))],
        out_shape=[jax.ShapeDtypeStruct((2, t, nb, s5w), F32),
                   jax.ShapeDtypeStruct((2, nslab, nb, sw2), F32)],
        scratch_shapes=[pltpu.VMEM((nslab, tt * nb, sw2), F32),
                        pltpu.VMEM((nslab, nb, sw2), F32)],
        compiler_params=_cparams("arbitrary", "arbitrary"),
        name="s5_scan",
    )(u3, bm, cm, a_re, a_im, h0)


def _s5_operators(lam_re, lam_im, log_dt, b_re, b_im, c_re, c_im):
    lr, li = lam_re.astype(F32), lam_im.astype(F32)
    dt = jnp.exp(log_dt.astype(F32))[..., None]
    mag = jnp.exp(lr * dt)
    ab_re, ab_im = mag * jnp.cos(li * dt), mag * jnp.sin(li * dt)
    den = lr * lr + li * li
    zr = ab_re - 1.0
    q_re = (zr * lr + ab_im * li) / den
    q_im = (ab_im * lr - zr * li) / den
    br, bi = b_re.astype(F32), b_im.astype(F32)
    bb_re = q_re[..., None] * br - q_im[..., None] * bi
    bb_im = q_re[..., None] * bi + q_im[..., None] * br
    two, g, p, h = bb_re.shape
    gs = S5_SLAB_GROUPS
    nslab = g // gs
    eye = jnp.eye(gs, dtype=F32)

    def in_mat(bb):
        bb = bb.reshape(two, nslab, gs, p, h)
        m = jnp.einsum('dmgph,gk->dmghkp', bb, eye)
        return m.reshape(two, nslab, gs * h, gs * p)

    def out_mat(cc):
        cc = cc.astype(F32).reshape(two, nslab, gs, h, p)
        m = jnp.einsum('dmghp,gk->dmgpkh', cc, eye)
        return m.reshape(two, nslab, gs * p, gs * h)

    bm = jnp.concatenate([in_mat(bb_re), in_mat(bb_im)], axis=-1).astype(BF16)
    cm = jnp.concatenate([out_mat(c_re), -out_mat(c_im)], axis=-2).astype(BF16)
    a_re = ab_re.reshape(two, nslab, 1, gs * p)
    a_im = ab_im.reshape(two, nslab, 1, gs * p)
    return bm, cm, a_re, a_im


def _lru_kernel(x_ref, xp_ref, xn_ref, cw_ref, cb_ref, c1_ref, wg_ref, ba_ref, bx_ref, h0_ref,
                y_ref, ht_ref, xpad_ref, xc_ref, a_ref, b_ref, st_ref, *, cb, r, nb):
    d = pl.program_id(0)
    j = pl.program_id(1)
    nj = pl.num_programs(1)
    jeff = j + d * (nj - 1 - 2 * j)
    steps = cb * r
    rows = steps * nb
    w = x_ref.shape[-1]

    @pl.when(j == 0)
    def _():
        st_ref[...] = h0_ref[0]

    has_prev = (jeff > 0).astype(F32)
    has_next = (jeff < nj - 1).astype(F32)
    xpad_ref[0:nb, :] = xp_ref[0, 0].astype(F32) * has_prev
    xpad_ref[nb:nb + rows, :] = x_ref[...].reshape(rows, w).astype(F32)
    xpad_ref[nb + rows:3 * nb + rows, :] = xn_ref[0].reshape(2 * nb, w).astype(F32) * has_next
    acc = cb_ref[...] + cw_ref[0:1, :] * xpad_ref[0:rows, :]
    for k in range(1, CONV_W):
        acc = acc + cw_ref[k:k + 1, :] * xpad_ref[k * nb:k * nb + rows, :]
    xc_ref[...] = acc

    for s in range(w // LANES):
        ls = slice(s * LANES, (s + 1) * LANES)
        xs = xc_ref[:, ls]
        gs = jnp.dot(xs.astype(BF16), wg_ref[0, s], preferred_element_type=F32)
        tr = jnp.tanh(gs[:, :LANES] + ba_ref[0, :, ls])
        ig = 0.5 * jnp.tanh(gs[:, LANES:] + bx_ref[0, :, ls]) + 0.5
        c1 = c1_ref[0, :, ls]
        a = jnp.exp(c1 * tr + c1)
        a_ref[:, ls] = a
        s1 = 1.0 - a * a
        root = jnp.where(s1 > 0.0, s1 * lax.rsqrt(s1), 0.0)
        b_ref[:, ls] = root * (ig * xs)

    def step(i, h):
        t = jnp.where(d == 0, i, steps - 1 - i)
        r0 = pl.multiple_of(t * nb, nb)
        h = a_ref[pl.ds(r0, nb), :] * h + b_ref[pl.ds(r0, nb), :]
        y_ref[0, t // r, t % r] = h
        return h

    st_ref[...] = lax.fori_loop(0, steps, step, st_ref[...])

    @pl.when(j == nj - 1)
    def _():
        ht_ref[0] = st_ref[...]


def _lru_scan(x4, conv_w, conv_b, sp, wg, b_a, b_x, h0, *, cb):
    c, r, nb, w = x4.shape
    nj = c // cb
    rows = cb * r * nb
    blk = lambda d, j: j + d * (nj - 1 - 2 * j)
    full2 = lambda a: pl.BlockSpec(a.shape, lambda d, j: (0,) * a.ndim)
    perdir = lambda a: pl.BlockSpec((1,) + a.shape[1:], lambda d, j: (d,) + (0,) * (a.ndim - 1))
    return pl.pallas_call(
        functools.partial(_lru_kernel, cb=cb, r=r, nb=nb),
        grid=(2, nj),
        in_specs=[pl.BlockSpec((cb, r, nb, w), lambda d, j: (blk(d, j), 0, 0, 0)),
                  pl.BlockSpec((1, 1, nb, w),
                               lambda d, j: (jnp.maximum(blk(d, j) * cb - 1, 0), r - 1, 0, 0)),
                  pl.BlockSpec((1, 2, nb, w),
                               lambda d, j: (jnp.minimum((blk(d, j) + 1) * cb, c - 1), 0, 0, 0)),
                  full2(conv_w), full2(conv_b), perdir(sp), perdir(wg), perdir(b_a), perdir(b_x),
                  perdir(h0)],
        out_specs=[pl.BlockSpec((1, cb, r, nb, w), lambda d, j: (d, blk(d, j), 0, 0, 0)),
                   pl.BlockSpec((1, nb, w), lambda d, j: (d, 0, 0))],
        out_shape=[jax.ShapeDtypeStruct((2, c, r, nb, w), F32),
                   jax.ShapeDtypeStruct((2, nb, w), F32)],
        scratch_shapes=[pltpu.VMEM((rows + 3 * nb, w), F32),
                        pltpu.VMEM((rows, w), F32),
                        pltpu.VMEM((rows, w), F32),
                        pltpu.VMEM((rows, w), F32),
                        pltpu.VMEM((nb, w), F32)],
        compiler_params=_cparams("arbitrary", "arbitrary"),
        name="lru_scan",
    )(x4, x4, x4, conv_w, conv_b, sp, wg, b_a, b_x, h0)


def _lru_gate_weights(w_a, w_x):
    two, nblk, k, _ = w_a.shape
    per = LANES // k
    eye = jnp.eye(per, dtype=F32)

    def slabs(wm):
        wm = wm.astype(F32).reshape(two, nblk // per, per, k, k)
        m = jnp.einsum('dsgij,gk->dsgikj', wm, eye)
        return m.reshape(two, nblk // per, LANES, LANES)

    return jnp.concatenate([slabs(w_a), slabs(w_x)], axis=-1).astype(BF16)


def _k2_kernel(x_ref, u_ref, ys_ref, yl_ref, gg_ref, m5_ref, ml_ref, d5_ref, wglu_ref, bglu_ref,
               wp5_ref, wpl_ref, wo_ref, g1_ref, gf_ref, sh_ref, sc_ref, wrh_ref, wrl_ref, br_ref,
               x1_ref, hn_ref, ti_ref, tw_ref, cnt_ref, *, nb):
    tm, dm = x_ref.shape
    u = u_ref[...]
    y5 = ys_ref[0] + ys_ref[1] + d5_ref[...] * u
    gy = _gelu(y5)
    o5 = gy * _sigmoid(jnp.dot(gy.astype(BF16), wglu_ref[...], preferred_element_type=F32)
                       + bglu_ref[...])
    yl = (yl_ref[0] + yl_ref[1]).reshape(tm, -1)
    ol = _gelu(gg_ref[...].astype(F32)) * yl
    merged = (_sigmoid(m5_ref[...]).astype(F32)
              * jnp.dot(o5.astype(BF16), wp5_ref[...], preferred_element_type=F32)
              + _sigmoid(ml_ref[...]).astype(F32)
              * jnp.dot(ol.astype(BF16), wpl_ref[...], preferred_element_type=F32))
    mix = jnp.dot(merged.astype(BF16), wo_ref[...], preferred_element_type=F32)
    x1 = (x_ref[...].reshape(tm // nb, nb, dm) + g1_ref[...][None] * mix.reshape(tm // nb, nb, dm))
    x1 = x1.reshape(tm, dm)
    x1_ref[...] = x1
    hn = _rms_mod(x1, gf_ref[...], sh_ref[...], sc_ref[...], nb)
    hn_hi = hn.astype(BF16)
    hn_ref[...] = hn_hi
    hn_lo = (hn - hn_hi.astype(F32)).astype(BF16)

    nt = (((1,), (1,)), ((), ()))
    logits = (lax.dot_general(wrh_ref[...], hn_hi, nt, preferred_element_type=F32)
              + lax.dot_general(wrl_ref[...], hn_hi, nt, preferred_element_type=F32)
              + lax.dot_general(wrh_ref[...], hn_lo, nt, preferred_element_type=F32)) + br_ref[...]
    ne = logits.shape[0]
    eidx = lax.broadcasted_iota(I32, (ne, tm), 0).astype(F32)
    vals = logits
    tv, ti = [], []
    for _ in range(TOP_K):
        m = jnp.max(vals, axis=0, keepdims=True)
        idx = jnp.min(jnp.where(vals == m, eidx, float(ne)), axis=0, keepdims=True)
        tv.append(m)
        ti.append(idx)
        vals = jnp.where(eidx == idx, -jnp.inf, vals)
    ex = [jnp.exp(v - tv[0]) for v in tv]
    den = ex[0] + ex[1] + ex[2] + ex[3]
    ti_ref[...] = jnp.concatenate(ti, axis=0).astype(I32)
    tw_ref[...] = jnp.concatenate([e / den for e in ex], axis=0)
    sel = (eidx == ti[0]) | (eidx == ti[1]) | (eidx == ti[2]) | (eidx == ti[3])
    self32 = jnp.where(sel, 1.0, 0.0)
    for sb in range(tm // SUB):
        cnt_ref[sb] = jnp.sum(self32[:, sb * SUB:(sb + 1) * SUB], axis=1, keepdims=True).astype(I32)


def _k2(x2, u, ys, yl5, gg, m5, ml, d5, wglu, bglu, wp5, wpl, wo, g1, gf, sh, sc, wr_hi, wr_lo, br,
        *, nb, r, colmajor):
    n, dm = x2.shape
    t = n // nb
    ncol = t // r
    lw = yl5.shape[-1]
    s5w = u.shape[-1]
    ne = wr_hi.shape[0]
    if colmajor:
        cbk = max(1, min(ncol, K2_TILE // nb))
        per_row = ncol // cbk
        tm = cbk * nb
        yl_spec = pl.BlockSpec((2, cbk, 1, nb, lw),
                               lambda i: (0, i % per_row, i // per_row, 0, 0))
    else:
        cbk = max(1, min(ncol, K2_TILE // (r * nb)))
        tm = cbk * r * nb
        yl_spec = pl.BlockSpec((2, cbk, r, nb, lw), lambda i: (0, i, 0, 0, 0))
    row = lambda w: pl.BlockSpec((tm, w), lambda i: (i, 0))
    full = lambda a: pl.BlockSpec(a.shape, lambda i: (0,) * a.ndim)
    ys2 = ys.reshape(2, n, s5w)
    params = (d5, wglu, bglu, wp5, wpl, wo, g1, gf, sh, sc, wr_hi, wr_lo, br)
    return pl.pallas_call(
        functools.partial(_k2_kernel, nb=nb),
        grid=(n // tm,),
        in_specs=[row(dm), row(s5w), pl.BlockSpec((2, tm, s5w), lambda i: (0, i, 0)), yl_spec,
                  row(lw), row(dm), row(dm)] + [full(p) for p in params],
        out_specs=[row(dm), row(dm),
                   pl.BlockSpec((TOP_K, tm), lambda i: (0, i)),
                   pl.BlockSpec((TOP_K, tm), lambda i: (0, i)),
                   pl.BlockSpec((tm // SUB, ne, 1), lambda i: (i, 0, 0))],
        out_shape=[jax.ShapeDtypeStruct((n, dm), F32),
                   jax.ShapeDtypeStruct((n, dm), BF16),
                   jax.ShapeDtypeStruct((TOP_K, n), I32),
                   jax.ShapeDtypeStruct((TOP_K, n), F32),
                   jax.ShapeDtypeStruct((n // SUB, ne, 1), I32)],
        compiler_params=_cparams("arbitrary"),
        name="k2_mix_out",
    )(x2, u, ys2, yl5, gg, m5, ml, *params)


def _chunk_copies(n, cdst_ref, base, local_ref, sorted_ref, sem, to_sorted):
    def per_chunk(c, _):
        g = pl.multiple_of(cdst_ref[base + c], SEG_PAD)
        lo = pl.multiple_of(c * SEG_PAD, SEG_PAD)
        loc = local_ref.at[pl.ds(lo, SEG_PAD)]
        srt = sorted_ref.at[pl.ds(g, SEG_PAD)]
        if to_sorted:
            pltpu.make_async_copy(loc, srt, sem).start()
        else:
            pltpu.make_async_copy(srt, loc, sem).start()
        return 0

    lax.fori_loop(0, n, per_chunk, 0)


def _wait_chunks(n, src_ref, dst_ref, sem):
    def one(c, _):
        pltpu.make_async_copy(src_ref.at[pl.ds(0, SEG_PAD)], dst_ref.at[pl.ds(0, SEG_PAD)],
                              sem).wait()
        return 0

    lax.fori_loop(0, n, one, 0)


def _dispatch_kernel(cdst_ref, nch_ref, tail0_ref, tailn_ref,
                     hn_ref, ti_ref, locv_ref, xs_ref, slot_ref, xg_ref, zero_ref, sem, tsem,
                     *, ne):
    s = pl.program_id(0)
    ns = pl.num_programs(0)
    buf = s % 2

    @pl.when(s == 0)
    def _():
        zero_ref[...] = jnp.zeros_like(zero_ref)

        def per_e(e, _):
            def per_chunk(c, _):
                d0 = pl.multiple_of(tail0_ref[e] + c * SEG_PAD, SEG_PAD)
                pltpu.make_async_copy(zero_ref, xs_ref.at[pl.ds(d0, SEG_PAD)], tsem).start()
                return 0
            lax.fori_loop(0, tailn_ref[e], per_chunk, 0)
            return 0
        lax.fori_loop(0, ne, per_e, 0)

        def per_e_wait(e, _):
            def per_chunk(c, _):
                pltpu.make_async_copy(zero_ref, xs_ref.at[pl.ds(0, SEG_PAD)], tsem).wait()
                return 0
            lax.fori_loop(0, tailn_ref[e], per_chunk, 0)
            return 0
        lax.fori_loop(0, ne, per_e_wait, 0)

    ti = ti_ref[...].astype(F32)
    eidx = lax.broadcasted_iota(I32, (ne, SUB), 0).astype(F32)
    hit = [eidx == ti[k:k + 1] for k in range(TOP_K)]
    sel = hit[0] | hit[1] | hit[2] | hit[3]
    upper = (lax.broadcasted_iota(I32, (SUB, SUB), 0)
             < lax.broadcasted_iota(I32, (SUB, SUB), 1))
    rank = jnp.dot(jnp.where(sel, 1.0, 0.0).astype(BF16), jnp.where(upper, 1.0, 0.0).astype(BF16),
                   preferred_element_type=F32)
    slot_e = locv_ref[0].astype(F32) + rank
    slots = [jnp.sum(jnp.where(hit[k], slot_e, 0.0), axis=0, keepdims=True) for k in range(TOP_K)]
    slot_ref[...] = jnp.concatenate(slots, axis=0).astype(I32)

    hn = hn_ref[...]
    blk = 256
    for rb in range(SLOT_ROWS // blk):
        sidx = (lax.broadcasted_iota(I32, (blk, SUB), 0) + rb * blk).astype(F32)
        p = (sidx == slots[0]) | (sidx == slots[1]) | (sidx == slots[2]) | (sidx == slots[3])
        xg = jnp.dot(jnp.where(p, 1.0, 0.0).astype(BF16), hn, preferred_element_type=F32)
        xg_ref[buf, rb * blk:(rb + 1) * blk, :] = xg.astype(BF16)

    _chunk_copies(nch_ref[s], cdst_ref, s * MAX_CHUNKS, xg_ref.at[buf], xs_ref, sem.at[buf], True)

    @pl.when(s > 0)
    def _():
        _wait_chunks(nch_ref[s - 1], xg_ref.at[1 - buf], xs_ref, sem.at[1 - buf])

    @pl.when(s == ns - 1)
    def _():
        _wait_chunks(nch_ref[s], xg_ref.at[buf], xs_ref, sem.at[buf])


def _dispatch(hn, ti, meta, *, ne):
    n, dm = hn.shape
    ns = n // SUB
    gs = pltpu.PrefetchScalarGridSpec(
        num_scalar_prefetch=4,
        grid=(ns,),
        in_specs=[pl.BlockSpec((SUB, dm), lambda s, *_: (s, 0)),
                  pl.BlockSpec((TOP_K, SUB), lambda s, *_: (0, s)),
                  pl.BlockSpec((1, ne, 1), lambda s, *_: (s, 0, 0))],
        out_specs=[pl.BlockSpec(memory_space=pl.ANY),
                   pl.BlockSpec((TOP_K, SUB), lambda s, *_: (0, s))],
        scratch_shapes=[pltpu.VMEM((2, SLOT_ROWS, dm), BF16),
                        pltpu.VMEM((SEG_PAD, dm), BF16),
                        pltpu.SemaphoreType.DMA((2,)),
                        pltpu.SemaphoreType.DMA(())],
    )
    return pl.pallas_call(
        functools.partial(_dispatch_kernel, ne=ne),
        grid_spec=gs,
        out_shape=[jax.ShapeDtypeStruct((meta['p_max'], dm), BF16),
                   jax.ShapeDtypeStruct((TOP_K, n), I32)],
        compiler_params=_cparams("arbitrary"),
        name="moe_dispatch",
    )(meta['cdst'], meta['nch'], meta['tail0'], meta['tailn'], hn, ti, meta['locv'])


def _expert_kernel(te_ref, nt_ref, x_ref, wu_ref, b1_ref, wd_ref, bd_ref, y_ref,
                   w1_ref, w2_ref, act_ref):
    i = pl.program_id(0)
    live = i < nt_ref[0]
    new_expert = (i == 0) | (te_ref[jnp.maximum(i - 1, 0)] != te_ref[i])
    pair = 2 * LANES
    nblk = wu_ref.shape[-1] // pair

    @pl.when(live & new_expert)
    def _():
        rr = lax.broadcasted_iota(I32, (pair, pair), 0)
        cc = lax.broadcasted_iota(I32, (pair, pair), 1)
        src = jnp.where(cc < LANES, 2 * cc, 2 * (cc - LANES) + 1)
        perm = jnp.where(rr == src, 1.0, 0.0).astype(BF16)
        for jb in range(nblk):
            cols = slice(jb * pair, (jb + 1) * pair)
            w1_ref[:, cols] = jnp.dot(wu_ref[0, 0, :, cols].astype(BF16), perm,
                                      preferred_element_type=F32).astype(BF16)
        w2_ref[...] = wd_ref[0, 0].astype(BF16)

    @pl.when(live)
    def _():
        x = x_ref[...]
        for jb in range(nblk):
            cols = slice(jb * pair, (jb + 1) * pair)
            hb = jnp.dot(x, w1_ref[:, cols], preferred_element_type=F32) + b1_ref[0, :, cols]
            gate = jnp.minimum(hb[:, :LANES], SWIGLU_LIMIT)
            up = jnp.clip(hb[:, LANES:], -SWIGLU_LIMIT, SWIGLU_LIMIT)
            act = gate * _sigmoid(SWIGLU_ALPHA * gate) * (up + 1.0)
            act_ref[:, jb * LANES:(jb + 1) * LANES] = act.astype(BF16)
        y = jnp.dot(act_ref[...], w2_ref[...], preferred_element_type=F32) + bd_ref[0, 0]
        y_ref[...] = y.astype(y_ref.dtype)


def _experts(xs, w_up, b1p, w_down, b_down, meta, layer):
    p_max, dm = xs.shape
    _, ne, _, f2 = w_up.shape
    f = f2 // 2
    n_tiles = p_max // EXPERT_TILE
    tile = lambda i, te, nt: (jnp.minimum(i, jnp.maximum(nt[0] - 1, 0)), 0)
    byexp = lambda i, te, nt: (te[i], 0, 0)
    bylayer = lambda i, te, nt: (layer, te[i], 0, 0)
    gs = pltpu.PrefetchScalarGridSpec(
        num_scalar_prefetch=2,
        grid=(n_tiles,),
        in_specs=[pl.BlockSpec((EXPERT_TILE, dm), tile),
                  pl.BlockSpec((1, 1, dm, f2), bylayer), pl.BlockSpec((1, 1, f2), byexp),
                  pl.BlockSpec((1, 1, f, dm), bylayer), pl.BlockSpec((1, 1, 1, dm), bylayer)],
        out_specs=pl.BlockSpec((EXPERT_TILE, dm), tile),
        scratch_shapes=[pltpu.VMEM((dm, f2), BF16), pltpu.VMEM((f, dm), BF16),
                        pltpu.VMEM((EXPERT_TILE, f), BF16)],
    )
    return pl.pallas_call(
        _expert_kernel,
        grid_spec=gs,
        out_shape=jax.ShapeDtypeStruct((p_max, dm), BF16),
        compiler_params=_cparams("arbitrary"),
        name="moe_experts",
    )(meta['tile_e'], meta['n_tiles'], xs, w_up, b1p, w_down, b_down)


def _combine_kernel(cdst_ref, nch_ref, x_ref, slot_ref, tw_ref, g2_ref, gfin_ref,
                    ys_ref, o_ref, yl_ref, sem, *, s0, nb, final_norm):
    i = pl.program_id(0)
    ni = pl.num_programs(0)
    buf = i % 2

    def fetch(step, b):
        _chunk_copies(nch_ref[s0 + step], cdst_ref, (s0 + step) * MAX_CHUNKS, yl_ref.at[b], ys_ref,
                      sem.at[b], False)

    @pl.when(i == 0)
    def _():
        yl_ref[...] = jnp.zeros_like(yl_ref)
        fetch(0, 0)

    @pl.when(i + 1 < ni)
    def _():
        fetch(i + 1, 1 - buf)

    _wait_chunks(nch_ref[s0 + i], ys_ref, yl_ref.at[buf], sem.at[buf])

    slot = slot_ref[...].astype(F32)
    tw = tw_ref[...]
    sidx = lax.broadcasted_iota(I32, (SUB, SLOT_ROWS), 1).astype(F32)
    pw = jnp.where(sidx == slot[:, 0:1], tw[:, 0:1], 0.0)
    for k in range(1, TOP_K):
        pw = pw + jnp.where(sidx == slot[:, k:k + 1], tw[:, k:k + 1], 0.0)
    f = jnp.dot(pw.astype(BF16), yl_ref[buf], preferred_element_type=F32)
    dm = f.shape[-1]
    x2 = x_ref[...].reshape(SUB // nb, nb, dm) + g2_ref[...][None] * f.reshape(SUB // nb, nb, dm)
    x2 = x2.reshape(SUB, dm)
    if final_norm:
        x2 = x2 * lax.rsqrt(jnp.mean(x2 * x2, axis=-1, keepdims=True) + EPS) * gfin_ref[...]
    o_ref[...] = x2


def _combine(x2d, slot_t, tw_t, g2, gfin, ys, meta, *, s0, nb, final_norm):
    n, dm = x2d.shape
    ns = n // SUB
    gs = pltpu.PrefetchScalarGridSpec(
        num_scalar_prefetch=2,
        grid=(ns,),
        in_specs=[pl.BlockSpec((SUB, dm), lambda i, *_: (i, 0)),
                  pl.BlockSpec((SUB, TOP_K), lambda i, *_: (s0 + i, 0)),
                  pl.BlockSpec((SUB, TOP_K), lambda i, *_: (s0 + i, 0)),
                  pl.BlockSpec(g2.shape, lambda i, *_: (0, 0)),
                  pl.BlockSpec(gfin.shape, lambda i, *_: (0, 0)),
                  pl.BlockSpec(memory_space=pl.ANY)],
        out_specs=pl.BlockSpec((SUB, dm), lambda i, *_: (i, 0)),
        scratch_shapes=[pltpu.VMEM((2, SLOT_ROWS, dm), BF16),
                        pltpu.SemaphoreType.DMA((2,))],
    )
    return pl.pallas_call(
        functools.partial(_combine_kernel, s0=s0, nb=nb, final_norm=final_norm),
        grid_spec=gs,
        out_shape=jax.ShapeDtypeStruct((n, dm), F32),
        compiler_params=_cparams("arbitrary"),
        name="moe_combine",
    )(meta['cdst'], meta['nch'], x2d, slot_t, tw_t, g2, gfin, ys)


def _routing_meta(cnt, ne):
    ns = cnt.shape[0]
    pc = (cnt + (SEG_PAD - 1)) // SEG_PAD * SEG_PAD
    loc = jnp.cumsum(pc, axis=1) - pc
    tot = jnp.sum(pc, axis=0)
    reg = (tot + (EXPERT_TILE - 1)) // EXPERT_TILE * EXPERT_TILE
    reg_end = jnp.cumsum(reg)
    base = reg_end - reg
    glob = base[None, :] + jnp.cumsum(pc, axis=0) - pc
    p_max = ns * (SUB * TOP_K + ne * (SEG_PAD - 1)) + ne * (EXPERT_TILE - SEG_PAD)
    p_max = (p_max + EXPERT_TILE - 1) // EXPERT_TILE * EXPERT_TILE
    n_tiles_max = p_max // EXPERT_TILE
    tile_start = jnp.arange(n_tiles_max, dtype=I32) * EXPERT_TILE
    tile_e = jnp.sum((tile_start[:, None] >= reg_end[None, :]).astype(I32), axis=1)
    tile_e = jnp.minimum(tile_e, ne - 1)
    crow = jnp.arange(MAX_CHUNKS, dtype=I32) * SEG_PAD
    loc_end = loc + pc
    ce = jnp.sum((crow[None, :, None] >= loc_end[:, None, :]).astype(I32), axis=2)
    ce = jnp.minimum(ce, ne - 1)
    pick = ce[:, :, None] == jnp.arange(ne, dtype=I32)[None, None, :]
    cdst = crow[None, :] + jnp.sum(jnp.where(pick, (glob - loc)[:, None, :], 0), axis=2)
    return dict(cdst=cdst.reshape(-1).astype(I32),
                nch=(jnp.sum(pc, axis=1) // SEG_PAD).astype(I32),
                tail0=(base + tot).astype(I32), tailn=((reg - tot) // SEG_PAD).astype(I32),
                locv=loc.reshape(ns, ne, 1).astype(I32),
                tile_e=tile_e, n_tiles=(reg_end[-1:] // EXPERT_TILE).astype(I32), p_max=p_max)


def kernel(x, c, ctx, c_ctx, w_ada, b_ada, g_mix, g_ffn, w_in, s5_lam_re, s5_lam_im, s5_log_dt, s5_b_re, s5_b_im, s5_c_re, s5_c_im, s5_d, s5_w_glu, s5_b_glu, lru_conv_w, lru_conv_b, lru_lam, lru_w_a, lru_b_a, lru_w_x, lru_b_x, w_proj_s5, w_proj_lru, w_out, w_router, b_router, w_up, b_up, w_down, b_down, g_final):
    nb, seq, dm = x.shape
    tc = ctx.shape[1]
    depth = w_ada.shape[0]
    r = seq // GRID_W
    s5w = s5_d.shape[-1]
    lw = lru_conv_w.shape[-1]
    ne = w_router.shape[-1]
    nl, ncx = seq * nb, tc * nb

    xl = jnp.transpose(x, (1, 0, 2)).reshape(nl, dm)
    xc = jnp.transpose(ctx, (1, 0, 2)).reshape(ncx, dm)

    pad_rows = (-(nb + 1)) % 8
    cvec = jnp.concatenate([c, c_ctx[None, :], jnp.zeros((pad_rows, dm), F32)], axis=0)
    ada = _ada_all(cvec, w_ada, b_ada)

    for l in range(depth):
        need_ctx = l < depth - 1
        mods_l = [ada[l, :nb, k * dm:(k + 1) * dm] for k in range(6)]
        mods_c = [jnp.broadcast_to(ada[l, nb:nb + 1, k * dm:(k + 1) * dm], (nb, dm)) for k in range(6)]
        gm = g_mix[l][None, :]
        gf = g_ffn[l][None, :]
        w_in_l = w_in[l].astype(BF16)

        u_l, xr_l, gg_l, m5_l, ml_l = _k1(xl, gm, mods_l[0], mods_l[1], w_in_l, nb=nb, r=r,
                                          colmajor=True, s5w=s5w, lw=lw, need_gates=True)
        outs_c = _k1(xc, gm, mods_c[0], mods_c[1], w_in_l, nb=nb, r=r, colmajor=False,
                     s5w=s5w, lw=lw, need_gates=need_ctx)
        u_c, xr_c = outs_c[0], outs_c[1]

        bm, cm, a_re, a_im = _s5_operators(s5_lam_re[l], s5_lam_im[l], s5_log_dt[l], s5_b_re[l],
                                           s5_b_im[l], s5_c_re[l], s5_c_im[l])
        nslab = s5w // LANES
        h0 = jnp.zeros((2, nslab, nb, 2 * S5_SLAB_STATE), F32)
        ys_c, hs_c = _s5_scan(u_c.reshape(tc, nb, s5w), bm, cm, a_re, a_im, h0)
        ys_l, _ = _s5_scan(u_l.reshape(seq, nb, s5w), bm, cm, a_re, a_im, hs_c)

        lam = lru_lam[l].astype(F32)
        softplus = jnp.maximum(-lam, 0.0) + jnp.log1p(jnp.exp(-jnp.abs(lam)))
        sp = (-0.5 * LRU_C) * softplus[:, None, :]
        wg = _lru_gate_weights(0.5 * lru_w_a[l], 0.5 * lru_w_x[l])
        cw = lru_conv_w[l]
        cbias = lru_conv_b[l][None, :]
        b_a = 0.5 * lru_b_a[l][:, None, :]
        b_x = 0.5 * lru_b_x[l][:, None, :]
        hz = jnp.zeros((2, nb, lw), F32)
        cb_c = 2 if (tc // r) % 2 == 0 else 1
        cb_l = 2 if GRID_W % 2 == 0 else 1
        yl_c, hl_c = _lru_scan(xr_c, cw, cbias, sp, wg, b_a, b_x, hz, cb=cb_c)
        yl_l, _ = _lru_scan(xr_l, cw, cbias, sp, wg, b_a, b_x, hl_c, cb=cb_l)

        k2_params = (s5_d[l][None, :], s5_w_glu[l].astype(BF16), s5_b_glu[l][None, :],
                     w_proj_s5[l].astype(BF16), w_proj_lru[l].astype(BF16), w_out[l].astype(BF16))
        wr_t = jnp.transpose(w_router[l])
        wr_hi = wr_t.astype(BF16)
        wr_lo = (wr_t - wr_hi.astype(F32)).astype(BF16)
        br = b_router[l][:, None]
        x1_l, hn_l, ti_l, tw_l, cnt_l = _k2(xl, u_l, ys_l, yl_l, gg_l, m5_l, ml_l, *k2_params,
                                            mods_l[2], gf, mods_l[3], mods_l[4], wr_hi, wr_lo, br,
                                            nb=nb, r=r, colmajor=True)
        if need_ctx:
            _, _, gg_c, m5_c, ml_c = outs_c
            x1_c, hn_c, ti_c, tw_c, cnt_c = _k2(xc, u_c, ys_c, yl_c, gg_c, m5_c, ml_c, *k2_params,
                                                mods_c[2], gf, mods_c[3], mods_c[4], wr_hi, wr_lo, br,
                                                nb=nb, r=r, colmajor=False)
            hn = jnp.concatenate([hn_c, hn_l], axis=0)
            ti = jnp.concatenate([ti_c, ti_l], axis=1)
            tw = jnp.concatenate([tw_c, tw_l], axis=1)
            cnt = jnp.concatenate([cnt_c, cnt_l], axis=0)
        else:
            hn, ti, tw, cnt = hn_l, ti_l, tw_l, cnt_l

        meta = _routing_meta(cnt[:, :, 0], ne)
        xs, slots = _dispatch(hn, ti, meta, ne=ne)
        f2 = w_up.shape[-1]
        b1p = jnp.transpose(b_up[l].reshape(ne, f2 // (2 * LANES), LANES, 2), (0, 1, 3, 2))
        ys = _experts(xs, w_up, b1p.reshape(ne, 1, f2), w_down, b_down[:, :, None, :], meta, l)
        slot_t = jnp.transpose(slots)
        tw_t = jnp.transpose(tw)
        last = l == depth - 1
        gfin = g_final[None, :]
        if need_ctx:
            xc = _combine(x1_c, slot_t, tw_t, mods_c[5], gfin, ys, meta, s0=0, nb=nb,
                          final_norm=False)
            xl = _combine(x1_l, slot_t, tw_t, mods_l[5], gfin, ys, meta, s0=ncx // SUB,
                          nb=nb, final_norm=False)
        else:
            xl = _combine(x1_l, slot_t, tw_t, mods_l[5], gfin, ys, meta, s0=0, nb=nb,
                          final_norm=last)

    return jnp.transpose(xl.reshape(seq, nb, dm), (1, 0, 2)).astype(x.dtype)
```

```python
import functools
import math

import jax
import jax.numpy as jnp
from jax import lax
from jax.experimental import pallas as pl
from jax.experimental.pallas import tpu as pltpu

F32 = jnp.float32
BF16 = jnp.bfloat16
I32 = jnp.int32
HIGHEST = lax.Precision.HIGHEST

GRID_W = 64
S5_GROUP = 16
S5_STATE = 64
S5_DT_MIN = 0.001
S5_DT_MAX = 0.1
LRU_BLOCK = 64
LRU_C = 8.0
CONV_W = 4
TOP_K = 4
SWIGLU_LIMIT = 7.0
SWIGLU_ALPHA = 1.702
EPS = 1e-6

LANES = 128
BF16_ROWS = 16
VMEM_LIMIT = 56 * 1024 * 1024

ROW_TILE = 1024
K2_TILE = 512
S5_SLAB_GROUPS = LANES // S5_GROUP
S5_SLAB_STATE = S5_SLAB_GROUPS * S5_STATE
SUB = 256
SEG_PAD = BF16_ROWS
SLOT_ROWS = 1536
MAX_CHUNKS = SLOT_ROWS // SEG_PAD
EXPERT_TILE = 1024


def _cparams(*sem, **kw):
    return pltpu.CompilerParams(dimension_semantics=sem, vmem_limit_bytes=VMEM_LIMIT, **kw)


def _gelu(x):
    return 0.5 * x * (1.0 + jnp.tanh(math.sqrt(2.0 / math.pi) * (x + 0.044715 * (x * x * x))))


def _sigmoid(x):
    return 0.5 * jnp.tanh(0.5 * x) + 0.5


def _rms_mod(x, g, sh, sc, nb):
    tm, d = x.shape
    y = x * lax.rsqrt(jnp.mean(x * x, axis=-1, keepdims=True) + EPS) * g
    y3 = y.reshape(tm // nb, nb, d) * (1.0 + sc)[None] + sh[None]
    return y3.reshape(tm, d)


def _ada_kernel(c_ref, w_ref, b_ref, o_ref):
    c = c_ref[...]
    cond = c * _sigmoid(c)
    o_ref[0] = jnp.dot(cond, w_ref[0], preferred_element_type=F32, precision=HIGHEST) + b_ref[0]


def _ada_all(cvec, w_ada, b_ada):
    nl, d, w6 = w_ada.shape
    r = cvec.shape[0]
    tn = min(w6, 1536)
    return pl.pallas_call(
        _ada_kernel,
        grid=(nl, w6 // tn),
        in_specs=[pl.BlockSpec((r, d), lambda l, j: (0, 0)),
                  pl.BlockSpec((1, d, tn), lambda l, j: (l, 0, j)),
                  pl.BlockSpec((1, 1, tn), lambda l, j: (l, 0, j))],
        out_specs=pl.BlockSpec((1, r, tn), lambda l, j: (l, 0, j)),
        out_shape=jax.ShapeDtypeStruct((nl, r, w6), F32),
        compiler_params=_cparams("arbitrary", "arbitrary"),
        name="ada",
    )(cvec, w_ada, b_ada.reshape(nl, 1, w6))


def _k1(x2, g, sh, sc, w_in, *, nb, r, colmajor, s5w, lw, need_gates):
    n, d = x2.shape
    t = n // nb
    ncol = t // r
    if colmajor:
        tm = (t // r) * nb
        xr_spec = pl.BlockSpec((ncol, 1, nb, lw), lambda i: (0, i, 0, 0))
    else:
        cbk = max(1, min(ncol, ROW_TILE // (r * nb)))
        tm = cbk * r * nb
        xr_spec = pl.BlockSpec((cbk, r, nb, lw), lambda i: (i, 0, 0, 0))
    splits = (s5w, s5w + lw, s5w + 2 * lw, s5w + 2 * lw + d, s5w + 2 * lw + 2 * d)
    if not need_gates:
        splits = splits[:2]
    row = lambda w: pl.BlockSpec((tm, w), lambda i: (i, 0))
    full = lambda a: pl.BlockSpec(a.shape, lambda i: (0,) * a.ndim)
    out_specs = [row(s5w), xr_spec, row(lw), row(d), row(d)]
    out_shape = [jax.ShapeDtypeStruct((n, s5w), F32),
                 jax.ShapeDtypeStruct((ncol, r, nb, lw), F32),
                 jax.ShapeDtypeStruct((n, lw), BF16),
                 jax.ShapeDtypeStruct((n, d), BF16),
                 jax.ShapeDtypeStruct((n, d), BF16)]
    k = len(splits)

    def body(x_ref, g_ref, sh_ref, sc_ref, w_ref, *o_refs):
        hn = _rms_mod(x_ref[...], g_ref[...], sh_ref[...], sc_ref[...], nb).astype(BF16)
        lo = 0
        for o_ref, hi in zip(o_refs, splits):
            v = jnp.dot(hn, w_ref[:, lo:hi], preferred_element_type=F32).astype(o_ref.dtype)
            o_ref[...] = v.reshape(o_ref.shape)
            lo = hi

    w_used = w_in[:, :splits[-1]]
    return pl.pallas_call(
        body,
        grid=(n // tm,),
        in_specs=[row(d), full(g), full(sh), full(sc), full(w_used)],
        out_specs=out_specs[:k],
        out_shape=out_shape[:k],
        compiler_params=_cparams("arbitrary"),
        name="k1_in_proj",
    )(x2, g, sh, sc, w_used)


def _s5_kernel(u_ref, bm_ref, cm_ref, are_ref, aim_ref, h0_ref, y_ref, ht_ref, hs_ref, st_ref,
               *, tt, nb, nslab):
    d = pl.program_id(0)
    j = pl.program_id(1)
    nj = pl.num_programs(1)
    sw = S5_SLAB_STATE
    rows = tt * nb

    @pl.when(j == 0)
    def _():
        st_ref[...] = h0_ref[0]

    def run(reverse):
        for m in range(nslab):
            ub = u_ref[:, :, m * LANES:(m + 1) * LANES].reshape(rows, LANES).astype(BF16)
            hs_ref[m] = jnp.dot(ub, bm_ref[0, m], preferred_element_type=F32)
            a_re = jnp.broadcast_to(are_ref[0, m], (nb, sw))
            a_im = jnp.broadcast_to(aim_ref[0, m], (nb, sw))
            hr = st_ref[m, :, :sw]
            hi = st_ref[m, :, sw:]
            for i in range(tt):
                t = tt - 1 - i if reverse else i
                rs = slice(t * nb, (t + 1) * nb)
                nr = a_re * hr - a_im * hi + hs_ref[m, rs, :sw]
                ni = a_re * hi + a_im * hr + hs_ref[m, rs, sw:]
                hs_ref[m, rs, :sw] = nr
                hs_ref[m, rs, sw:] = ni
                hr, hi = nr, ni
            st_ref[m, :, :sw] = hr
            st_ref[m, :, sw:] = hi
            y = jnp.dot(hs_ref[m].astype(BF16), cm_ref[0, m], preferred_element_type=F32)
            y_ref[0, :, :, m * LANES:(m + 1) * LANES] = y.reshape(tt, nb, LANES)

    @pl.when(d == 0)
    def _():
        run(False)

    @pl.when(d == 1)
    def _():
        run(True)

    @pl.when(j == nj - 1)
    def _():
        ht_ref[0] = st_ref[...]


def _s5_scan(u3, bm, cm, a_re, a_im, h0):
    t, nb, s5w = u3.shape
    nslab = s5w // LANES
    tt = 64 if t % 64 == 0 else t
    nj = t // tt
    sw2 = 2 * S5_SLAB_STATE
    blk = lambda d, j: j + d * (nj - 1 - 2 * j)
    return pl.pallas_call(
        functools.partial(_s5_kernel, tt=tt, nb=nb, nslab=nslab),
        grid=(2, nj),
        in_specs=[pl.BlockSpec((tt, nb, s5w), lambda d, j: (blk(d, j), 0, 0)),
                  pl.BlockSpec((1, nslab, LANES, sw2), lambda d, j: (d, 0, 0, 0)),
                  pl.BlockSpec((1, nslab, sw2, LANES), lambda d, j: (d, 0, 0, 0)),
                  pl.BlockSpec((1, nslab, 1, S5_SLAB_STATE), lambda d, j: (d, 0, 0, 0)),
                  pl.BlockSpec((1, nslab, 1, S5_SLAB_STATE), lambda d, j: (d, 0, 0, 0)),
                  pl.BlockSpec((1, nslab, nb, sw2), lambda d, j: (d, 0, 0, 0))],
        out_specs=[pl.BlockSpec((1, tt, nb, s5w), lambda d, j: (d, blk(d, j), 0, 0)),
                   pl.BlockSpec((1, nslab, nb, sw2), lambda d, j: (d, 0, 0, 0))],
        out_shape=[jax.ShapeDtypeStruct((2, t, nb, s5w), F32),
                   jax.ShapeDtypeStruct((2, nslab, nb, sw2), F32)],
        scratch_shapes=[pltpu.VMEM((nslab, tt * nb, sw2), F32),
                        pltpu.VMEM((nslab, nb, sw2), F32)],
        compiler_params=_cparams("arbitrary", "arbitrary"),
        name="s5_scan",
    )(u3, bm, cm, a_re, a_im, h0)


def _s5_operators(lam_re, lam_im, log_dt, b_re, b_im, c_re, c_im):
    lr, li = lam_re.astype(F32), lam_im.astype(F32)
    dt = jnp.exp(log_dt.astype(F32))[..., None]
    mag = jnp.exp(lr * dt)
    ab_re, ab_im = mag * jnp.cos(li * dt), mag * jnp.sin(li * dt)
    den = lr * lr + li * li
    zr = ab_re - 1.0
    q_re = (zr * lr + ab_im * li) / den
    q_im = (ab_im * lr - zr * li) / den
    br, bi = b_re.astype(F32), b_im.astype(F32)
    bb_re = q_re[..., None] * br - q_im[..., None] * bi
    bb_im = q_re[..., None] * bi + q_im[..., None] * br
    two, g, p, h = bb_re.shape
    gs = S5_SLAB_GROUPS
    nslab = g // gs
    eye = jnp.eye(gs, dtype=F32)

    def in_mat(bb):
        bb = bb.reshape(two, nslab, gs, p, h)
        m = jnp.einsum('dmgph,gk->dmghkp', bb, eye)
        return m.reshape(two, nslab, gs * h, gs * p)

    def out_mat(cc):
        cc = cc.astype(F32).reshape(two, nslab, gs, h, p)
        m = jnp.einsum('dmghp,gk->dmgpkh', cc, eye)
        return m.reshape(two, nslab, gs * p, gs * h)

    bm = jnp.concatenate([in_mat(bb_re), in_mat(bb_im)], axis=-1).astype(BF16)
    cm = jnp.concatenate([out_mat(c_re), -out_mat(c_im)], axis=-2).astype(BF16)
    a_re = ab_re.reshape(two, nslab, 1, gs * p)
    a_im = ab_im.reshape(two, nslab, 1, gs * p)
    return bm, cm, a_re, a_im


def _lru_kernel(x_ref, xp_ref, xn_ref, cw_ref, cb_ref, c1_ref, wg_ref, ba_ref, bx_ref, h0_ref,
                y_ref, ht_ref, xpad_ref, xc_ref, a_ref, b_ref, st_ref, *, cb, r, nb):
    d = pl.program_id(0)
    j = pl.program_id(1)
    nj = pl.num_programs(1)
    jeff = j + d * (nj - 1 - 2 * j)
    steps = cb * r
    rows = steps * nb
    w = x_ref.shape[-1]

    @pl.when(j == 0)
    def _():
        st_ref[...] = h0_ref[0]

    has_prev = (jeff > 0).astype(F32)
    has_next = (jeff < nj - 1).astype(F32)
    xpad_ref[0:nb, :] = xp_ref[0, 0].astype(F32) * has_prev
    xpad_ref[nb:nb + rows, :] = x_ref[...].reshape(rows, w).astype(F32)
    xpad_ref[nb + rows:3 * nb + rows, :] = xn_ref[0].reshape(2 * nb, w).astype(F32) * has_next
    acc = cb_ref[...] + cw_ref[0:1, :] * xpad_ref[0:rows, :]
    for k in range(1, CONV_W):
        acc = acc + cw_ref[k:k + 1, :] * xpad_ref[k * nb:k * nb + rows, :]
    xc_ref[...] = acc

    for s in range(w // LANES):
        ls = slice(s * LANES, (s + 1) * LANES)
        xs = xc_ref[:, ls]
        gs = jnp.dot(xs.astype(BF16), wg_ref[0, s], preferred_element_type=F32)
        tr = jnp.tanh(gs[:, :LANES] + ba_ref[0, :, ls])
        ig = 0.5 * jnp.tanh(gs[:, LANES:] + bx_ref[0, :, ls]) + 0.5
        c1 = c1_ref[0, :, ls]
        a = jnp.exp(c1 * tr + c1)
        a_ref[:, ls] = a
        s1 = 1.0 - a * a
        root = jnp.where(s1 > 0.0, s1 * lax.rsqrt(s1), 0.0)
        b_ref[:, ls] = root * (ig * xs)

    def step(i, h):
        t = jnp.where(d == 0, i, steps - 1 - i)
        r0 = pl.multiple_of(t * nb, nb)
        h = a_ref[pl.ds(r0, nb), :] * h + b_ref[pl.ds(r0, nb), :]
        y_ref[0, t // r, t % r] = h
        return h

    st_ref[...] = lax.fori_loop(0, steps, step, st_ref[...])

    @pl.when(j == nj - 1)
    def _():
        ht_ref[0] = st_ref[...]


def _lru_scan(x4, conv_w, conv_b, sp, wg, b_a, b_x, h0, *, cb):
    c, r, nb, w = x4.shape
    nj = c // cb
    rows = cb * r * nb
    blk = lambda d, j: j + d * (nj - 1 - 2 * j)
    full2 = lambda a: pl.BlockSpec(a.shape, lambda d, j: (0,) * a.ndim)
    perdir = lambda a: pl.BlockSpec((1,) + a.shape[1:], lambda d, j: (d,) + (0,) * (a.ndim - 1))
    return pl.pallas_call(
        functools.partial(_lru_kernel, cb=cb, r=r, nb=nb),
        grid=(2, nj),
        in_specs=[pl.BlockSpec((cb, r, nb, w), lambda d, j: (blk(d, j), 0, 0, 0)),
                  pl.BlockSpec((1, 1, nb, w),
                               lambda d, j: (jnp.maximum(blk(d, j) * cb - 1, 0), r - 1, 0, 0)),
                  pl.BlockSpec((1, 2, nb, w),
                               lambda d, j: (jnp.minimum((blk(d, j) + 1) * cb, c - 1), 0, 0, 0)),
                  full2(conv_w), full2(conv_b), perdir(sp), perdir(wg), perdir(b_a), perdir(b_x),
                  perdir(h0)],
        out_specs=[pl.BlockSpec((1, cb, r, nb, w), lambda d, j: (d, blk(d, j), 0, 0, 0)),
                   pl.BlockSpec((1, nb, w), lambda d, j: (d, 0, 0))],
        out_shape=[jax.ShapeDtypeStruct((2, c, r, nb, w), F32),
                   jax.ShapeDtypeStruct((2, nb, w), F32)],
        scratch_shapes=[pltpu.VMEM((rows + 3 * nb, w), F32),
                        pltpu.VMEM((rows, w), F32),
                        pltpu.VMEM((rows, w), F32),
                        pltpu.VMEM((rows, w), F32),
                        pltpu.VMEM((nb, w), F32)],
        compiler_params=_cparams("arbitrary", "arbitrary"),
        name="lru_scan",
    )(x4, x4, x4, conv_w, conv_b, sp, wg, b_a, b_x, h0)


def _lru_gate_weights(w_a, w_x):
    two, nblk, k, _ = w_a.shape
    per = LANES // k
    eye = jnp.eye(per, dtype=F32)

    def slabs(wm):
        wm = wm.astype(F32).reshape(two, nblk // per, per, k, k)
        m = jnp.einsum('dsgij,gk->dsgikj', wm, eye)
        return m.reshape(two, nblk // per, LANES, LANES)

    return jnp.concatenate([slabs(w_a), slabs(w_x)], axis=-1).astype(BF16)


def _k2_kernel(x_ref, u_ref, ys_ref, yl_ref, gg_ref, m5_ref, ml_ref, d5_ref, wglu_ref, bglu_ref,
               wp5_ref, wpl_ref, wo_ref, g1_ref, gf_ref, sh_ref, sc_ref, wrh_ref, wrl_ref, br_ref,
               x1_ref, hn_ref, ti_ref, tw_ref, cnt_ref, *, nb):
    tm, dm = x_ref.shape
    u = u_ref[...]
    y5 = ys_ref[0] + ys_ref[1] + d5_ref[...] * u
    gy = _gelu(y5)
    o5 = gy * _sigmoid(jnp.dot(gy.astype(BF16), wglu_ref[...], preferred_element_type=F32)
                       + bglu_ref[...])
    yl = (yl_ref[0] + yl_ref[1]).reshape(tm, -1)
    ol = _gelu(gg_ref[...].astype(F32)) * yl
    merged = (_sigmoid(m5_ref[...]).astype(F32)
              * jnp.dot(o5.astype(BF16), wp5_ref[...], preferred_element_type=F32)
              + _sigmoid(ml_ref[...]).astype(F32)
              * jnp.dot(ol.astype(BF16), wpl_ref[...], preferred_element_type=F32))
    mix = jnp.dot(merged.astype(BF16), wo_ref[...], preferred_element_type=F32)
    x1 = (x_ref[...].reshape(tm // nb, nb, dm) + g1_ref[...][None] * mix.reshape(tm // nb, nb, dm))
    x1 = x1.reshape(tm, dm)
    x1_ref[...] = x1
    hn = _rms_mod(x1, gf_ref[...], sh_ref[...], sc_ref[...], nb)
    hn_hi = hn.astype(BF16)
    hn_ref[...] = hn_hi
    hn_lo = (hn - hn_hi.astype(F32)).astype(BF16)

    nt = (((1,), (1,)), ((), ()))
    logits = (lax.dot_general(wrh_ref[...], hn_hi, nt, preferred_element_type=F32)
              + lax.dot_general(wrl_ref[...], hn_hi, nt, preferred_element_type=F32)
              + lax.dot_general(wrh_ref[...], hn_lo, nt, preferred_element_type=F32)) + br_ref[...]
    ne = logits.shape[0]
    eidx = lax.broadcasted_iota(I32, (ne, tm), 0).astype(F32)
    vals = logits
    tv, ti = [], []
    for _ in range(TOP_K):
        m = jnp.max(vals, axis=0, keepdims=True)
        idx = jnp.min(jnp.where(vals == m, eidx, float(ne)), axis=0, keepdims=True)
        tv.append(m)
        ti.append(idx)
        vals = jnp.where(eidx == idx, -jnp.inf, vals)
    ex = [jnp.exp(v - tv[0]) for v in tv]
    den = ex[0] + ex[1] + ex[2] + ex[3]
    ti_ref[...] = jnp.concatenate(ti, axis=0).astype(I32)
    tw_ref[...] = jnp.concatenate([e / den for e in ex], axis=0)
    sel = (eidx == ti[0]) | (eidx == ti[1]) | (eidx == ti[2]) | (eidx == ti[3])
    self32 = jnp.where(sel, 1.0, 0.0)
    for sb in range(tm // SUB):
        cnt_ref[sb] = jnp.sum(self32[:, sb * SUB:(sb + 1) * SUB], axis=1, keepdims=True).astype(I32)


def _k2(x2, u, ys, yl5, gg, m5, ml, d5, wglu, bglu, wp5, wpl, wo, g1, gf, sh, sc, wr_hi, wr_lo, br,
        *, nb, r, colmajor):
    n, dm = x2.shape
    t = n // nb
    ncol = t // r
    lw = yl5.shape[-1]
    s5w = u.shape[-1]
    ne = wr_hi.shape[0]
    if colmajor:
        cbk = max(1, min(ncol, K2_TILE // nb))
        per_row = ncol // cbk
        tm = cbk * nb
        yl_spec = pl.BlockSpec((2, cbk, 1, nb, lw),
                               lambda i: (0, i % per_row, i // per_row, 0, 0))
    else:
        cbk = max(1, min(ncol, K2_TILE // (r * nb)))
        tm = cbk * r * nb
        yl_spec = pl.BlockSpec((2, cbk, r, nb, lw), lambda i: (0, i, 0, 0, 0))
    row = lambda w: pl.BlockSpec((tm, w), lambda i: (i, 0))
    full = lambda a: pl.BlockSpec(a.shape, lambda i: (0,) * a.ndim)
    ys2 = ys.reshape(2, n, s5w)
    params = (d5, wglu, bglu, wp5, wpl, wo, g1, gf, sh, sc, wr_hi, wr_lo, br)
    return pl.pallas_call(
        functools.partial(_k2_kernel, nb=nb),
        grid=(n // tm,),
        in_specs=[row(dm), row(s5w), pl.BlockSpec((2, tm, s5w), lambda i: (0, i, 0)), yl_spec,
                  row(lw), row(dm), row(dm)] + [full(p) for p in params],
        out_specs=[row(dm), row(dm),
                   pl.BlockSpec((TOP_K, tm), lambda i: (0, i)),
                   pl.BlockSpec((TOP_K, tm), lambda i: (0, i)),
                   pl.BlockSpec((tm // SUB, ne, 1), lambda i: (i, 0, 0))],
        out_shape=[jax.ShapeDtypeStruct((n, dm), F32),
                   jax.ShapeDtypeStruct((n, dm), BF16),
                   jax.ShapeDtypeStruct((TOP_K, n), I32),
                   jax.ShapeDtypeStruct((TOP_K, n), F32),
                   jax.ShapeDtypeStruct((n // SUB, ne, 1), I32)],
        compiler_params=_cparams("arbitrary"),
        name="k2_mix_out",
    )(x2, u, ys2, yl5, gg, m5, ml, *params)


def _chunk_copies(n, cdst_ref, base, local_ref, sorted_ref, sem, to_sorted):
    def per_chunk(c, _):
        g = pl.multiple_of(cdst_ref[base + c], SEG_PAD)
        lo = pl.multiple_of(c * SEG_PAD, SEG_PAD)
        loc = local_ref.at[pl.ds(lo, SEG_PAD)]
        srt = sorted_ref.at[pl.ds(g, SEG_PAD)]
        if to_sorted:
            pltpu.make_async_copy(loc, srt, sem).start()
        else:
            pltpu.make_async_copy(srt, loc, sem).start()
        return 0

    lax.fori_loop(0, n, per_chunk, 0)


def _wait_chunks(n, src_ref, dst_ref, sem):
    def one(c, _):
        pltpu.make_async_copy(src_ref.at[pl.ds(0, SEG_PAD)], dst_ref.at[pl.ds(0, SEG_PAD)],
                              sem).wait()
        return 0

    lax.fori_loop(0, n, one, 0)


def _dispatch_kernel(cdst_ref, nch_ref, tail0_ref, tailn_ref,
                     hn_ref, ti_ref, locv_ref, xs_ref, slot_ref, xg_ref, zero_ref, sem, tsem,
                     *, ne):
    s = pl.program_id(0)
    ns = pl.num_programs(0)
    buf = s % 2

    @pl.when(s == 0)
    def _():
        zero_ref[...] = jnp.zeros_like(zero_ref)

        def per_e(e, _):
            def per_chunk(c, _):
                d0 = pl.multiple_of(tail0_ref[e] + c * SEG_PAD, SEG_PAD)
                pltpu.make_async_copy(zero_ref, xs_ref.at[pl.ds(d0, SEG_PAD)], tsem).start()
                return 0
            lax.fori_loop(0, tailn_ref[e], per_chunk, 0)
            return 0
        lax.fori_loop(0, ne, per_e, 0)

        def per_e_wait(e, _):
            def per_chunk(c, _):
                pltpu.make_async_copy(zero_ref, xs_ref.at[pl.ds(0, SEG_PAD)], tsem).wait()
                return 0
            lax.fori_loop(0, tailn_ref[e], per_chunk, 0)
            return 0
        lax.fori_loop(0, ne, per_e_wait, 0)

    ti = ti_ref[...].astype(F32)
    eidx = lax.broadcasted_iota(I32, (ne, SUB), 0).astype(F32)
    hit = [eidx == ti[k:k + 1] for k in range(TOP_K)]
    sel = hit[0] | hit[1] | hit[2] | hit[3]
    upper = (lax.broadcasted_iota(I32, (SUB, SUB), 0)
             < lax.broadcasted_iota(I32, (SUB, SUB), 1))
    rank = jnp.dot(jnp.where(sel, 1.0, 0.0).astype(BF16), jnp.where(upper, 1.0, 0.0).astype(BF16),
                   preferred_element_type=F32)
    slot_e = locv_ref[0].astype(F32) + rank
    slots = [jnp.sum(jnp.where(hit[k], slot_e, 0.0), axis=0, keepdims=True) for k in range(TOP_K)]
    slot_ref[...] = jnp.concatenate(slots, axis=0).astype(I32)

    hn = hn_ref[...]
    blk = 256
    for rb in range(SLOT_ROWS // blk):
        sidx = (lax.broadcasted_iota(I32, (blk, SUB), 0) + rb * blk).astype(F32)
        p = (sidx == slots[0]) | (sidx == slots[1]) | (sidx == slots[2]) | (sidx == slots[3])
        xg = jnp.dot(jnp.where(p, 1.0, 0.0).astype(BF16), hn, preferred_element_type=F32)
        xg_ref[buf, rb * blk:(rb + 1) * blk, :] = xg.astype(BF16)

    _chunk_copies(nch_ref[s], cdst_ref, s * MAX_CHUNKS, xg_ref.at[buf], xs_ref, sem.at[buf], True)

    @pl.when(s > 0)
    def _():
        _wait_chunks(nch_ref[s - 1], xg_ref.at[1 - buf], xs_ref, sem.at[1 - buf])

    @pl.when(s == ns - 1)
    def _():
        _wait_chunks(nch_ref[s], xg_ref.at[buf], xs_ref, sem.at[buf])


def _dispatch(hn, ti, meta, *, ne):
    n, dm = hn.shape
    ns = n // SUB
    gs = pltpu.PrefetchScalarGridSpec(
        num_scalar_prefetch=4,
        grid=(ns,),
        in_specs=[pl.BlockSpec((SUB, dm), lambda s, *_: (s, 0)),
                  pl.BlockSpec((TOP_K, SUB), lambda s, *_: (0, s)),
                  pl.BlockSpec((1, ne, 1), lambda s, *_: (s, 0, 0))],
        out_specs=[pl.BlockSpec(memory_space=pl.ANY),
                   pl.BlockSpec((TOP_K, SUB), lambda s, *_: (0, s))],
        scratch_shapes=[pltpu.VMEM((2, SLOT_ROWS, dm), BF16),
                        pltpu.VMEM((SEG_PAD, dm), BF16),
                        pltpu.SemaphoreType.DMA((2,)),
                        pltpu.SemaphoreType.DMA(())],
    )
    return pl.pallas_call(
        functools.partial(_dispatch_kernel, ne=ne),
        grid_spec=gs,
        out_shape=[jax.ShapeDtypeStruct((meta['p_max'], dm), BF16),
                   jax.ShapeDtypeStruct((TOP_K, n), I32)],
        compiler_params=_cparams("arbitrary"),
        name="moe_dispatch",
    )(meta['cdst'], meta['nch'], meta['tail0'], meta['tailn'], hn, ti, meta['locv'])


def _expert_kernel(te_ref, nt_ref, run_ref, rexp_ref, nrun_ref, x_ref, b1_ref, bd_ref,
                   wu_hbm, wd_hbm, y_ref, wu_buf, wd_buf, w1_ref, w2_ref, act_ref, sem, *, layer):
    i = pl.program_id(0)
    live = i < nt_ref[0]
    run = run_ref[i]
    new_run = (i == 0) | (run_ref[jnp.maximum(i - 1, 0)] != run)
    pair = 2 * LANES
    nblk = wu_buf.shape[-1] // pair

    def weight_copies(expert, slot):
        return (pltpu.make_async_copy(wu_hbm.at[layer, expert], wu_buf.at[slot], sem.at[0, slot]),
                pltpu.make_async_copy(wd_hbm.at[layer, expert], wd_buf.at[slot], sem.at[1, slot]))

    @pl.when(live & new_run)
    def _():
        slot = run % 2

        @pl.when(run == 0)
        def _():
            for cp in weight_copies(rexp_ref[0], 0):
                cp.start()

        for cp in weight_copies(rexp_ref[run], slot):
            cp.wait()

        @pl.when(run + 1 < nrun_ref[0])
        def _():
            for cp in weight_copies(rexp_ref[run + 1], 1 - slot):
                cp.start()

        rr = lax.broadcasted_iota(I32, (pair, pair), 0)
        cc = lax.broadcasted_iota(I32, (pair, pair), 1)
        src = jnp.where(cc < LANES, 2 * cc, 2 * (cc - LANES) + 1)
        perm = jnp.where(rr == src, 1.0, 0.0).astype(BF16)
        for jb in range(nblk):
            cols = slice(jb * pair, (jb + 1) * pair)
            w1_ref[:, cols] = jnp.dot(wu_buf[slot, :, cols].astype(BF16), perm,
                                      preferred_element_type=F32).astype(BF16)
        w2_ref[...] = wd_buf[slot].astype(BF16)

    @pl.when(live)
    def _():
        x = x_ref[...]
        for jb in range(nblk):
            cols = slice(jb * pair, (jb + 1) * pair)
            hb = jnp.dot(x, w1_ref[:, cols], preferred_element_type=F32) + b1_ref[0, :, cols]
            gate = jnp.minimum(hb[:, :LANES], SWIGLU_LIMIT)
            up = jnp.clip(hb[:, LANES:], -SWIGLU_LIMIT, SWIGLU_LIMIT)
            act = gate * _sigmoid(SWIGLU_ALPHA * gate) * (up + 1.0)
            act_ref[:, jb * LANES:(jb + 1) * LANES] = act.astype(BF16)
        y = jnp.dot(act_ref[...], w2_ref[...], preferred_element_type=F32) + bd_ref[0, 0]
        y_ref[...] = y.astype(y_ref.dtype)


def _experts(xs, w_up, b1p, w_down, b_down, meta, layer):
    p_max, dm = xs.shape
    _, ne, _, f2 = w_up.shape
    f = f2 // 2
    n_tiles = p_max // EXPERT_TILE
    tile = lambda i, te, nt, *_: (jnp.minimum(i, jnp.maximum(nt[0] - 1, 0)), 0)
    byexp = lambda i, te, *_: (te[i], 0, 0)
    bylayer = lambda i, te, *_: (layer, te[i], 0, 0)
    gs = pltpu.PrefetchScalarGridSpec(
        num_scalar_prefetch=5,
        grid=(n_tiles,),
        in_specs=[pl.BlockSpec((EXPERT_TILE, dm), tile),
                  pl.BlockSpec((1, 1, f2), byexp), pl.BlockSpec((1, 1, 1, dm), bylayer),
                  pl.BlockSpec(memory_space=pl.ANY), pl.BlockSpec(memory_space=pl.ANY)],
        out_specs=pl.BlockSpec((EXPERT_TILE, dm), tile),
        scratch_shapes=[pltpu.VMEM((2, dm, f2), F32), pltpu.VMEM((2, f, dm), F32),
                        pltpu.VMEM((dm, f2), BF16), pltpu.VMEM((f, dm), BF16),
                        pltpu.VMEM((EXPERT_TILE, f), BF16),
                        pltpu.SemaphoreType.DMA((2, 2))],
    )
    return pl.pallas_call(
        functools.partial(_expert_kernel, layer=layer),
        grid_spec=gs,
        out_shape=jax.ShapeDtypeStruct((p_max, dm), BF16),
        compiler_params=_cparams("arbitrary"),
        name="moe_experts",
    )(meta['tile_e'], meta['n_tiles'], meta['tile_run'], meta['run_e'], meta['n_runs'],
      xs, b1p, b_down, w_up, w_down)


def _combine_kernel(cdst_ref, nch_ref, x_ref, slot_ref, tw_ref, g2_ref, gfin_ref,
                    ys_ref, o_ref, yl_ref, sem, *, s0, nb, final_norm):
    i = pl.program_id(0)
    ni = pl.num_programs(0)
    buf = i % 2

    def fetch(step, b):
        _chunk_copies(nch_ref[s0 + step], cdst_ref, (s0 + step) * MAX_CHUNKS, yl_ref.at[b], ys_ref,
                      sem.at[b], False)

    @pl.when(i == 0)
    def _():
        yl_ref[...] = jnp.zeros_like(yl_ref)
        fetch(0, 0)

    @pl.when(i + 1 < ni)
    def _():
        fetch(i + 1, 1 - buf)

    _wait_chunks(nch_ref[s0 + i], ys_ref, yl_ref.at[buf], sem.at[buf])

    slot = slot_ref[...].astype(F32)
    tw = tw_ref[...]
    sidx = lax.broadcasted_iota(I32, (SUB, SLOT_ROWS), 1).astype(F32)
    pw = jnp.where(sidx == slot[:, 0:1], tw[:, 0:1], 0.0)
    for k in range(1, TOP_K):
        pw = pw + jnp.where(sidx == slot[:, k:k + 1], tw[:, k:k + 1], 0.0)
    f = jnp.dot(pw.astype(BF16), yl_ref[buf], preferred_element_type=F32)
    dm = f.shape[-1]
    x2 = x_ref[...].reshape(SUB // nb, nb, dm) + g2_ref[...][None] * f.reshape(SUB // nb, nb, dm)
    x2 = x2.reshape(SUB, dm)
    if final_norm:
        x2 = x2 * lax.rsqrt(jnp.mean(x2 * x2, axis=-1, keepdims=True) + EPS) * gfin_ref[...]
    o_ref[...] = x2


def _combine(x2d, slot_t, tw_t, g2, gfin, ys, meta, *, s0, nb, final_norm):
    n, dm = x2d.shape
    ns = n // SUB
    gs = pltpu.PrefetchScalarGridSpec(
        num_scalar_prefetch=2,
        grid=(ns,),
        in_specs=[pl.BlockSpec((SUB, dm), lambda i, *_: (i, 0)),
                  pl.BlockSpec((SUB, TOP_K), lambda i, *_: (s0 + i, 0)),
                  pl.BlockSpec((SUB, TOP_K), lambda i, *_: (s0 + i, 0)),
                  pl.BlockSpec(g2.shape, lambda i, *_: (0, 0)),
                  pl.BlockSpec(gfin.shape, lambda i, *_: (0, 0)),
                  pl.BlockSpec(memory_space=pl.ANY)],
        out_specs=pl.BlockSpec((SUB, dm), lambda i, *_: (i, 0)),
        scratch_shapes=[pltpu.VMEM((2, SLOT_ROWS, dm), BF16),
                        pltpu.SemaphoreType.DMA((2,))],
    )
    return pl.pallas_call(
        functools.partial(_combine_kernel, s0=s0, nb=nb, final_norm=final_norm),
        grid_spec=gs,
        out_shape=jax.ShapeDtypeStruct((n, dm), F32),
        compiler_params=_cparams("arbitrary"),
        name="moe_combine",
    )(meta['cdst'], meta['nch'], x2d, slot_t, tw_t, g2, gfin, ys)


def _routing_meta(cnt, ne):
    ns = cnt.shape[0]
    pc = (cnt + (SEG_PAD - 1)) // SEG_PAD * SEG_PAD
    loc = jnp.cumsum(pc, axis=1) - pc
    tot = jnp.sum(pc, axis=0)
    reg = (tot + (EXPERT_TILE - 1)) // EXPERT_TILE * EXPERT_TILE
    reg_end = jnp.cumsum(reg)
    base = reg_end - reg
    glob = base[None, :] + jnp.cumsum(pc, axis=0) - pc
    p_max = ns * (SUB * TOP_K + ne * (SEG_PAD - 1)) + ne * (EXPERT_TILE - SEG_PAD)
    p_max = (p_max + EXPERT_TILE - 1) // EXPERT_TILE * EXPERT_TILE
    n_tiles_max = p_max // EXPERT_TILE
    tile_start = jnp.arange(n_tiles_max, dtype=I32) * EXPERT_TILE
    tile_e = jnp.sum((tile_start[:, None] >= reg_end[None, :]).astype(I32), axis=1)
    tile_e = jnp.minimum(tile_e, ne - 1)
    eids = jnp.arange(ne, dtype=I32)
    has = reg > 0
    run_of_e = jnp.cumsum(has.astype(I32)) - 1
    run_e = jnp.sum(jnp.where(has[None, :] & (run_of_e[None, :] == eids[:, None]), eids[None, :], 0),
                    axis=1)
    tile_run = jnp.sum(jnp.where(tile_e[:, None] == eids[None, :], run_of_e[None, :], 0), axis=1)
    crow = jnp.arange(MAX_CHUNKS, dtype=I32) * SEG_PAD
    loc_end = loc + pc
    ce = jnp.sum((crow[None, :, None] >= loc_end[:, None, :]).astype(I32), axis=2)
    ce = jnp.minimum(ce, ne - 1)
    pick = ce[:, :, None] == jnp.arange(ne, dtype=I32)[None, None, :]
    cdst = crow[None, :] + jnp.sum(jnp.where(pick, (glob - loc)[:, None, :], 0), axis=2)
    return dict(cdst=cdst.reshape(-1).astype(I32),
                nch=(jnp.sum(pc, axis=1) // SEG_PAD).astype(I32),
                tail0=(base + tot).astype(I32), tailn=((reg - tot) // SEG_PAD).astype(I32),
                locv=loc.reshape(ns, ne, 1).astype(I32),
                tile_e=tile_e, n_tiles=(reg_end[-1:] // EXPERT_TILE).astype(I32), p_max=p_max,
                tile_run=jnp.maximum(tile_run, 0).astype(I32), run_e=run_e.astype(I32),
                n_runs=jnp.sum(has.astype(I32))[None])


def kernel(x, c, ctx, c_ctx, w_ada, b_ada, g_mix, g_ffn, w_in, s5_lam_re, s5_lam_im, s5_log_dt, s5_b_re, s5_b_im, s5_c_re, s5_c_im, s5_d, s5_w_glu, s5_b_glu, lru_conv_w, lru_conv_b, lru_lam, lru_w_a, lru_b_a, lru_w_x, lru_b_x, w_proj_s5, w_proj_lru, w_out, w_router, b_router, w_up, b_up, w_down, b_down, g_final):
    nb, seq, dm = x.shape
    tc = ctx.shape[1]
    depth = w_ada.shape[0]
    r = seq // GRID_W
    s5w = s5_d.shape[-1]
    lw = lru_conv_w.shape[-1]
    ne = w_router.shape[-1]
    nl, ncx = seq * nb, tc * nb

    xl = jnp.transpose(x, (1, 0, 2)).reshape(nl, dm)
    xc = jnp.transpose(ctx, (1, 0, 2)).reshape(ncx, dm)

    pad_rows = (-(nb + 1)) % 8
    cvec = jnp.concatenate([c, c_ctx[None, :], jnp.zeros((pad_rows, dm), F32)], axis=0)
    ada = _ada_all(cvec, w_ada, b_ada)

    for l in range(depth):
        need_ctx = l < depth - 1
        mods_l = [ada[l, :nb, k * dm:(k + 1) * dm] for k in range(6)]
        mods_c = [jnp.broadcast_to(ada[l, nb:nb + 1, k * dm:(k + 1) * dm], (nb, dm)) for k in range(6)]
        gm = g_mix[l][None, :]
        gf = g_ffn[l][None, :]
        w_in_l = w_in[l].astype(BF16)

        u_l, xr_l, gg_l, m5_l, ml_l = _k1(xl, gm, mods_l[0], mods_l[1], w_in_l, nb=nb, r=r,
                                          colmajor=True, s5w=s5w, lw=lw, need_gates=True)
        outs_c = _k1(xc, gm, mods_c[0], mods_c[1], w_in_l, nb=nb, r=r, colmajor=False,
                     s5w=s5w, lw=lw, need_gates=need_ctx)
        u_c, xr_c = outs_c[0], outs_c[1]

        bm, cm, a_re, a_im = _s5_operators(s5_lam_re[l], s5_lam_im[l], s5_log_dt[l], s5_b_re[l],
                                           s5_b_im[l], s5_c_re[l], s5_c_im[l])
        nslab = s5w // LANES
        h0 = jnp.zeros((2, nslab, nb, 2 * S5_SLAB_STATE), F32)
        ys_c, hs_c = _s5_scan(u_c.reshape(tc, nb, s5w), bm, cm, a_re, a_im, h0)
        ys_l, _ = _s5_scan(u_l.reshape(seq, nb, s5w), bm, cm, a_re, a_im, hs_c)

        lam = lru_lam[l].astype(F32)
        softplus = jnp.maximum(-lam, 0.0) + jnp.log1p(jnp.exp(-jnp.abs(lam)))
        sp = (-0.5 * LRU_C) * softplus[:, None, :]
        wg = _lru_gate_weights(0.5 * lru_w_a[l], 0.5 * lru_w_x[l])
        cw = lru_conv_w[l]
        cbias = lru_conv_b[l][None, :]
        b_a = 0.5 * lru_b_a[l][:, None, :]
        b_x = 0.5 * lru_b_x[l][:, None, :]
        hz = jnp.zeros((2, nb, lw), F32)
        cb_c = 2 if (tc // r) % 2 == 0 else 1
        cb_l = 2 if GRID_W % 2 == 0 else 1
        yl_c, hl_c = _lru_scan(xr_c, cw, cbias, sp, wg, b_a, b_x, hz, cb=cb_c)
        yl_l, _ = _lru_scan(xr_l, cw, cbias, sp, wg, b_a, b_x, hl_c, cb=cb_l)

        k2_params = (s5_d[l][None, :], s5_w_glu[l].astype(BF16), s5_b_glu[l][None, :],
                     w_proj_s5[l].astype(BF16), w_proj_lru[l].astype(BF16), w_out[l].astype(BF16))
        wr_t = jnp.transpose(w_router[l])
        wr_hi = wr_t.astype(BF16)
        wr_lo = (wr_t - wr_hi.astype(F32)).astype(BF16)
        br = b_router[l][:, None]
        x1_l, hn_l, ti_l, tw_l, cnt_l = _k2(xl, u_l, ys_l, yl_l, gg_l, m5_l, ml_l, *k2_params,
                                            mods_l[2], gf, mods_l[3], mods_l[4], wr_hi, wr_lo, br,
                                            nb=nb, r=r, colmajor=True)
        if need_ctx:
            _, _, gg_c, m5_c, ml_c = outs_c
            x1_c, hn_c, ti_c, tw_c, cnt_c = _k2(xc, u_c, ys_c, yl_c, gg_c, m5_c, ml_c, *k2_params,
                                                mods_c[2], gf, mods_c[3], mods_c[4], wr_hi, wr_lo, br,
                                                nb=nb, r=r, colmajor=False)
            hn = jnp.concatenate([hn_c, hn_l], axis=0)
            ti = jnp.concatenate([ti_c, ti_l], axis=1)
            tw = jnp.concatenate([tw_c, tw_l], axis=1)
            cnt = jnp.concatenate([cnt_c, cnt_l], axis=0)
        else:
            hn, ti, tw, cnt = hn_l, ti_l, tw_l, cnt_l

        meta = _routing_meta(cnt[:, :, 0], ne)
        xs, slots = _dispatch(hn, ti, meta, ne=ne)
        f2 = w_up.shape[-1]
        b1p = jnp.transpose(b_up[l].reshape(ne, f2 // (2 * LANES), LANES, 2), (0, 1, 3, 2))
        ys = _experts(xs, w_up, b1p.reshape(ne, 1, f2), w_down, b_down[:, :, None, :], meta, l)
        slot_t = jnp.transpose(slots)
        tw_t = jnp.transpose(tw)
        last = l == depth - 1
        gfin = g_final[None, :]
        if need_ctx:
            xc = _combine(x1_c, slot_t, tw_t, mods_c[5], gfin, ys, meta, s0=0, nb=nb,
                          final_norm=False)
            xl = _combine(x1_l, slot_t, tw_t, mods_l[5], gfin, ys, meta, s0=ncx // SUB,
                          nb=nb, final_norm=False)
        else:
            xl = _combine(x1_l, slot_t, tw_t, mods_l[5], gfin, ys, meta, s0=0, nb=nb,
                          final_norm=last)

    return jnp.transpose(xl.reshape(seq, nb, dm), (1, 0, 2)).astype(x.dtype)
```

```python
import functools
import math

import jax
import jax.numpy as jnp
from jax import lax
from jax.experimental import pallas as pl
from jax.experimental.pallas import tpu as pltpu

F32 = jnp.float32
BF16 = jnp.bfloat16
I32 = jnp.int32
HIGHEST = lax.Precision.HIGHEST

GRID_W = 64
S5_GROUP = 16
S5_STATE = 64
S5_DT_MIN = 0.001
S5_DT_MAX = 0.1
LRU_BLOCK = 64
LRU_C = 8.0
CONV_W = 4
TOP_K = 4
SWIGLU_LIMIT = 7.0
SWIGLU_ALPHA = 1.702
EPS = 1e-6

LANES = 128
BF16_ROWS = 16
VMEM_LIMIT = 56 * 1024 * 1024

ROW_TILE = 1024
K2_TILE = 512
S5_SLAB_GROUPS = LANES // S5_GROUP
S5_SLAB_STATE = S5_SLAB_GROUPS * S5_STATE
SUB = 256
SEG_PAD = BF16_ROWS
SLOT_ROWS = 1536
MAX_CHUNKS = SLOT_ROWS // SEG_PAD
SHORT_ROWS = SUB * TOP_K + 256
EXPERT_TILE = 1024


def _cparams(*sem, **kw):
    return pltpu.CompilerParams(dimension_semantics=sem, vmem_limit_bytes=VMEM_LIMIT, **kw)


def _gelu(x):
    c0 = math.sqrt(2.0 / math.pi)
    half = 0.5 * x
    return half + half * jnp.tanh(x * (c0 + (c0 * 0.044715) * (x * x)))


def _sigmoid(x):
    return 0.5 * jnp.tanh(0.5 * x) + 0.5


def _rms_mod(x, g, sh, sc, nb):
    tm, d = x.shape
    y = x * lax.rsqrt(jnp.mean(x * x, axis=-1, keepdims=True) + EPS) * g
    y3 = y.reshape(tm // nb, nb, d) * (1.0 + sc)[None] + sh[None]
    return y3.reshape(tm, d)


def _ada_kernel(c_ref, w_ref, b_ref, o_ref):
    c = c_ref[...]
    cond = c * _sigmoid(c)
    o_ref[0] = jnp.dot(cond, w_ref[0], preferred_element_type=F32, precision=HIGHEST) + b_ref[0]


def _ada_all(cvec, w_ada, b_ada):
    nl, d, w6 = w_ada.shape
    r = cvec.shape[0]
    tn = min(w6, 1536)
    return pl.pallas_call(
        _ada_kernel,
        grid=(nl, w6 // tn),
        in_specs=[pl.BlockSpec((r, d), lambda l, j: (0, 0)),
                  pl.BlockSpec((1, d, tn), lambda l, j: (l, 0, j)),
                  pl.BlockSpec((1, 1, tn), lambda l, j: (l, 0, j))],
        out_specs=pl.BlockSpec((1, r, tn), lambda l, j: (l, 0, j)),
        out_shape=jax.ShapeDtypeStruct((nl, r, w6), F32),
        compiler_params=_cparams("arbitrary", "arbitrary"),
        name="ada",
    )(cvec, w_ada, b_ada.reshape(nl, 1, w6))


def _k1(x2, g, sh, sc, w_in, *, nb, r, colmajor, s5w, lw, need_gates):
    n, d = x2.shape
    t = n // nb
    ncol = t // r
    if colmajor:
        tm = (t // r) * nb
        xr_spec = pl.BlockSpec((ncol, 1, nb, lw), lambda i: (0, i, 0, 0))
    else:
        cbk = max(1, min(ncol, ROW_TILE // (r * nb)))
        tm = cbk * r * nb
        xr_spec = pl.BlockSpec((cbk, r, nb, lw), lambda i: (i, 0, 0, 0))
    splits = (s5w, s5w + lw, s5w + 2 * lw, s5w + 2 * lw + d, s5w + 2 * lw + 2 * d)
    if not need_gates:
        splits = splits[:2]
    row = lambda w: pl.BlockSpec((tm, w), lambda i: (i, 0))
    full = lambda a: pl.BlockSpec(a.shape, lambda i: (0,) * a.ndim)
    out_specs = [row(s5w), xr_spec, row(lw), row(d), row(d)]
    out_shape = [jax.ShapeDtypeStruct((n, s5w), F32),
                 jax.ShapeDtypeStruct((ncol, r, nb, lw), F32),
                 jax.ShapeDtypeStruct((n, lw), BF16),
                 jax.ShapeDtypeStruct((n, d), BF16),
                 jax.ShapeDtypeStruct((n, d), BF16)]
    k = len(splits)

    def body(x_ref, g_ref, sh_ref, sc_ref, w_ref, *o_refs):
        hn = _rms_mod(x_ref[...], g_ref[...], sh_ref[...], sc_ref[...], nb).astype(BF16)
        lo = 0
        for o_ref, hi in zip(o_refs, splits):
            v = jnp.dot(hn, w_ref[:, lo:hi], preferred_element_type=F32).astype(o_ref.dtype)
            o_ref[...] = v.reshape(o_ref.shape)
            lo = hi

    w_used = w_in[:, :splits[-1]]
    return pl.pallas_call(
        body,
        grid=(n // tm,),
        in_specs=[row(d), full(g), full(sh), full(sc), full(w_used)],
        out_specs=out_specs[:k],
        out_shape=out_shape[:k],
        compiler_params=_cparams("arbitrary"),
        name="k1_in_proj",
    )(x2, g, sh, sc, w_used)


def _s5_kernel(u_ref, bm_ref, cm_ref, are_ref, aim_ref, h0_ref, y_ref, ht_ref, hs_ref, st_ref,
               *, tt, nb, nslab):
    d = pl.program_id(0)
    j = pl.program_id(1)
    nj = pl.num_programs(1)
    sw = S5_SLAB_STATE
    rows = tt * nb

    @pl.when(j == 0)
    def _():
        st_ref[...] = h0_ref[0]

    def project_in(m):
        ub = u_ref[:, :, m * LANES:(m + 1) * LANES].reshape(rows, LANES).astype(BF16)
        hs_ref[m] = jnp.dot(ub, bm_ref[0, m], preferred_element_type=F32)

    def recur(m, reverse):
        a_re = jnp.broadcast_to(are_ref[0, m], (nb, sw))
        a_im = jnp.broadcast_to(aim_ref[0, m], (nb, sw))
        hr = st_ref[m, :, :sw]
        hi = st_ref[m, :, sw:]
        for i in range(tt):
            t = tt - 1 - i if reverse else i
            rs = slice(t * nb, (t + 1) * nb)
            nr = a_re * hr - a_im * hi + hs_ref[m, rs, :sw]
            ni = a_re * hi + a_im * hr + hs_ref[m, rs, sw:]
            hs_ref[m, rs, :sw] = nr
            hs_ref[m, rs, sw:] = ni
            hr, hi = nr, ni
        st_ref[m, :, :sw] = hr
        st_ref[m, :, sw:] = hi

    def project_out(m):
        y = jnp.dot(hs_ref[m].astype(BF16), cm_ref[0, m], preferred_element_type=F32)
        y_ref[0, :, :, m * LANES:(m + 1) * LANES] = y.reshape(tt, nb, LANES)

    def run(reverse):
        project_in(0)
        for m in range(nslab):
            if m + 1 < nslab:
                project_in(m + 1)
            recur(m, reverse)
            project_out(m)

    @pl.when(d == 0)
    def _():
        run(False)

    @pl.when(d == 1)
    def _():
        run(True)

    @pl.when(j == nj - 1)
    def _():
        ht_ref[0] = st_ref[...]


def _s5_scan(u3, bm, cm, a_re, a_im, h0):
    t, nb, s5w = u3.shape
    nslab = s5w // LANES
    tt = 64 if t % 64 == 0 else t
    nj = t // tt
    sw2 = 2 * S5_SLAB_STATE
    blk = lambda d, j: j + d * (nj - 1 - 2 * j)
    return pl.pallas_call(
        functools.partial(_s5_kernel, tt=tt, nb=nb, nslab=nslab),
        grid=(2, nj),
        in_specs=[pl.BlockSpec((tt, nb, s5w), lambda d, j: (blk(d, j), 0, 0)),
                  pl.BlockSpec((1, nslab, LANES, sw2), lambda d, j: (d, 0, 0, 0)),
                  pl.BlockSpec((1, nslab, sw2, LANES), lambda d, j: (d, 0, 0, 0)),
                  pl.BlockSpec((1, nslab, 1, S5_SLAB_STATE), lambda d, j: (d, 0, 0, 0)),
                  pl.BlockSpec((1, nslab, 1, S5_SLAB_STATE), lambda d, j: (d, 0, 0, 0)),
                  pl.BlockSpec((1, nslab, nb, sw2), lambda d, j: (d, 0, 0, 0))],
        out_specs=[pl.BlockSpec((1, tt, nb, s5w), lambda d, j: (d, blk(d, j), 0, 0)),
                   pl.BlockSpec((1, nslab, nb, sw2), lambda d, j: (d, 0, 0, 0))],
        out_shape=[jax.ShapeDtypeStruct((2, t, nb, s5w), F32),
                   jax.ShapeDtypeStruct((2, nslab, nb, sw2), F32)],
        scratch_shapes=[pltpu.VMEM((nslab, tt * nb, sw2), F32),
                        pltpu.VMEM((nslab, nb, sw2), F32)],
        compiler_params=_cparams("arbitrary", "arbitrary"),
        name="s5_scan",
    )(u3, bm, cm, a_re, a_im, h0)


def _s5_operators(lam_re, lam_im, log_dt, b_re, b_im, c_re, c_im):
    lr, li = lam_re.astype(F32), lam_im.astype(F32)
    dt = jnp.exp(log_dt.astype(F32))[..., None]
    mag = jnp.exp(lr * dt)
    ab_re, ab_im = mag * jnp.cos(li * dt), mag * jnp.sin(li * dt)
    den = lr * lr + li * li
    zr = ab_re - 1.0
    q_re = (zr * lr + ab_im * li) / den
    q_im = (ab_im * lr - zr * li) / den
    br, bi = b_re.astype(F32), b_im.astype(F32)
    bb_re = q_re[..., None] * br - q_im[..., None] * bi
    bb_im = q_re[..., None] * bi + q_im[..., None] * br
    two, g, p, h = bb_re.shape
    gs = S5_SLAB_GROUPS
    nslab = g // gs
    eye = jnp.eye(gs, dtype=F32)

    def in_mat(bb):
        bb = bb.reshape(two, nslab, gs, p, h)
        m = jnp.einsum('dmgph,gk->dmghkp', bb, eye)
        return m.reshape(two, nslab, gs * h, gs * p)

    def out_mat(cc):
        cc = cc.astype(F32).reshape(two, nslab, gs, h, p)
        m = jnp.einsum('dmghp,gk->dmgpkh', cc, eye)
        return m.reshape(two, nslab, gs * p, gs * h)

    bm = jnp.concatenate([in_mat(bb_re), in_mat(bb_im)], axis=-1).astype(BF16)
    cm = jnp.concatenate([out_mat(c_re), -out_mat(c_im)], axis=-2).astype(BF16)
    a_re = ab_re.reshape(two, nslab, 1, gs * p)
    a_im = ab_im.reshape(two, nslab, 1, gs * p)
    return bm, cm, a_re, a_im


def _lru_kernel(x_ref, xp_ref, xn_ref, cw_ref, cb_ref, c1_ref, wg_ref, ba_ref, bx_ref, h0_ref,
                y_ref, ht_ref, xpad_ref, xc_ref, a_ref, b_ref, st_ref, *, cb, r, nb):
    d = pl.program_id(0)
    j = pl.program_id(1)
    nj = pl.num_programs(1)
    jeff = j + d * (nj - 1 - 2 * j)
    steps = cb * r
    rows = steps * nb
    w = x_ref.shape[-1]

    @pl.when(j == 0)
    def _():
        st_ref[...] = h0_ref[0]

    has_prev = (jeff > 0).astype(F32)
    has_next = (jeff < nj - 1).astype(F32)
    xpad_ref[0:nb, :] = xp_ref[0, 0].astype(F32) * has_prev
    xpad_ref[nb:nb + rows, :] = x_ref[...].reshape(rows, w).astype(F32)
    xpad_ref[nb + rows:3 * nb + rows, :] = xn_ref[0].reshape(2 * nb, w).astype(F32) * has_next
    acc = cb_ref[...] + cw_ref[0:1, :] * xpad_ref[0:rows, :]
    for k in range(1, CONV_W):
        acc = acc + cw_ref[k:k + 1, :] * xpad_ref[k * nb:k * nb + rows, :]
    xc_ref[...] = acc

    for s in range(w // LANES):
        ls = slice(s * LANES, (s + 1) * LANES)
        xs = xc_ref[:, ls]
        gs = jnp.dot(xs.astype(BF16), wg_ref[0, s], preferred_element_type=F32)
        tr = jnp.tanh(gs[:, :LANES] + ba_ref[0, :, ls])
        ig = 0.5 * jnp.tanh(gs[:, LANES:] + bx_ref[0, :, ls]) + 0.5
        c1 = c1_ref[0, :, ls]
        a = jnp.exp(c1 * tr + c1)
        a_ref[:, ls] = a
        s1 = 1.0 - a * a
        root = jnp.where(s1 > 0.0, s1 * lax.rsqrt(s1), 0.0)
        b_ref[:, ls] = root * (ig * xs)

    def step(i, h):
        t = jnp.where(d == 0, i, steps - 1 - i)
        r0 = pl.multiple_of(t * nb, nb)
        h = a_ref[pl.ds(r0, nb), :] * h + b_ref[pl.ds(r0, nb), :]
        y_ref[0, t // r, t % r] = h
        return h

    st_ref[...] = lax.fori_loop(0, steps, step, st_ref[...], unroll=4)

    @pl.when(j == nj - 1)
    def _():
        ht_ref[0] = st_ref[...]


def _lru_scan(x4, conv_w, conv_b, sp, wg, b_a, b_x, h0, *, cb):
    c, r, nb, w = x4.shape
    nj = c // cb
    rows = cb * r * nb
    blk = lambda d, j: j + d * (nj - 1 - 2 * j)
    full2 = lambda a: pl.BlockSpec(a.shape, lambda d, j: (0,) * a.ndim)
    perdir = lambda a: pl.BlockSpec((1,) + a.shape[1:], lambda d, j: (d,) + (0,) * (a.ndim - 1))
    return pl.pallas_call(
        functools.partial(_lru_kernel, cb=cb, r=r, nb=nb),
        grid=(2, nj),
        in_specs=[pl.BlockSpec((cb, r, nb, w), lambda d, j: (blk(d, j), 0, 0, 0)),
                  pl.BlockSpec((1, 1, nb, w),
                               lambda d, j: (jnp.maximum(blk(d, j) * cb - 1, 0), r - 1, 0, 0)),
                  pl.BlockSpec((1, 2, nb, w),
                               lambda d, j: (jnp.minimum((blk(d, j) + 1) * cb, c - 1), 0, 0, 0)),
                  full2(conv_w), full2(conv_b), perdir(sp), perdir(wg), perdir(b_a), perdir(b_x),
                  perdir(h0)],
        out_specs=[pl.BlockSpec((1, cb, r, nb, w), lambda d, j: (d, blk(d, j), 0, 0, 0)),
                   pl.BlockSpec((1, nb, w), lambda d, j: (d, 0, 0))],
        out_shape=[jax.ShapeDtypeStruct((2, c, r, nb, w), F32),
                   jax.ShapeDtypeStruct((2, nb, w), F32)],
        scratch_shapes=[pltpu.VMEM((rows + 3 * nb, w), F32),
                        pltpu.VMEM((rows, w), F32),
                        pltpu.VMEM((rows, w), F32),
                        pltpu.VMEM((rows, w), F32),
                        pltpu.VMEM((nb, w), F32)],
        compiler_params=_cparams("arbitrary", "arbitrary"),
        name="lru_scan",
    )(x4, x4, x4, conv_w, conv_b, sp, wg, b_a, b_x, h0)


def _lru_gate_weights(w_a, w_x):
    two, nblk, k, _ = w_a.shape
    per = LANES // k
    eye = jnp.eye(per, dtype=F32)

    def slabs(wm):
        wm = wm.astype(F32).reshape(two, nblk // per, per, k, k)
        m = jnp.einsum('dsgij,gk->dsgikj', wm, eye)
        return m.reshape(two, nblk // per, LANES, LANES)

    return jnp.concatenate([slabs(w_a), slabs(w_x)], axis=-1).astype(BF16)


def _k2_kernel(x_ref, u_ref, ys_ref, yl_ref, gg_ref, m5_ref, ml_ref, d5_ref, wglu_ref, bglu_ref,
               wp5_ref, wpl_ref, wo_ref, g1_ref, gf_ref, sh_ref, sc_ref, wrh_ref, wrl_ref, br_ref,
               x1_ref, hn_ref, ti_ref, tw_ref, cnt_ref, *, nb):
    tm, dm = x_ref.shape
    u = u_ref[...]
    y5 = ys_ref[0] + ys_ref[1] + d5_ref[...] * u
    gy = _gelu(y5)
    o5 = gy * _sigmoid(jnp.dot(gy.astype(BF16), wglu_ref[...], preferred_element_type=F32)
                       + bglu_ref[...])
    yl = (yl_ref[0] + yl_ref[1]).reshape(tm, -1)
    ol = _gelu(gg_ref[...].astype(F32)) * yl
    merged = (_sigmoid(m5_ref[...]).astype(F32)
              * jnp.dot(o5.astype(BF16), wp5_ref[...], preferred_element_type=F32)
              + _sigmoid(ml_ref[...]).astype(F32)
              * jnp.dot(ol.astype(BF16), wpl_ref[...], preferred_element_type=F32))
    mix = jnp.dot(merged.astype(BF16), wo_ref[...], preferred_element_type=F32)
    x1 = (x_ref[...].reshape(tm // nb, nb, dm) + g1_ref[...][None] * mix.reshape(tm // nb, nb, dm))
    x1 = x1.reshape(tm, dm)
    x1_ref[...] = x1
    hn = _rms_mod(x1, gf_ref[...], sh_ref[...], sc_ref[...], nb)
    hn_hi = hn.astype(BF16)
    hn_ref[...] = hn_hi
    hn_lo = (hn - hn_hi.astype(F32)).astype(BF16)

    nt = (((1,), (1,)), ((), ()))
    logits = (lax.dot_general(wrh_ref[...], hn_hi, nt, preferred_element_type=F32)
              + lax.dot_general(wrl_ref[...], hn_hi, nt, preferred_element_type=F32)
              + lax.dot_general(wrh_ref[...], hn_lo, nt, preferred_element_type=F32)) + br_ref[...]
    ne = logits.shape[0]
    eidx = lax.broadcasted_iota(I32, (ne, tm), 0).astype(F32)
    vals = logits
    tv, ti = [], []
    for _ in range(TOP_K):
        m = jnp.max(vals, axis=0, keepdims=True)
        idx = jnp.min(jnp.where(vals == m, eidx, float(ne)), axis=0, keepdims=True)
        tv.append(m)
        ti.append(idx)
        vals = jnp.where(eidx == idx, -jnp.inf, vals)
    ex = [jnp.exp(v - tv[0]) for v in tv]
    den = ex[0] + ex[1] + ex[2] + ex[3]
    ti_ref[...] = jnp.concatenate(ti, axis=0).astype(I32)
    tw_ref[...] = jnp.concatenate([e / den for e in ex], axis=0)
    sel = (eidx == ti[0]) | (eidx == ti[1]) | (eidx == ti[2]) | (eidx == ti[3])
    self32 = jnp.where(sel, 1.0, 0.0)
    for sb in range(tm // SUB):
        cnt_ref[sb] = jnp.sum(self32[:, sb * SUB:(sb + 1) * SUB], axis=1, keepdims=True).astype(I32)


def _k2(x2, u, ys, yl5, gg, m5, ml, d5, wglu, bglu, wp5, wpl, wo, g1, gf, sh, sc, wr_hi, wr_lo, br,
        *, nb, r, colmajor):
    n, dm = x2.shape
    t = n // nb
    ncol = t // r
    lw = yl5.shape[-1]
    s5w = u.shape[-1]
    ne = wr_hi.shape[0]
    if colmajor:
        cbk = max(1, min(ncol, K2_TILE // nb))
        per_row = ncol // cbk
        tm = cbk * nb
        yl_spec = pl.BlockSpec((2, cbk, 1, nb, lw),
                               lambda i: (0, i % per_row, i // per_row, 0, 0))
    else:
        cbk = max(1, min(ncol, K2_TILE // (r * nb)))
        tm = cbk * r * nb
        yl_spec = pl.BlockSpec((2, cbk, r, nb, lw), lambda i: (0, i, 0, 0, 0))
    row = lambda w: pl.BlockSpec((tm, w), lambda i: (i, 0))
    full = lambda a: pl.BlockSpec(a.shape, lambda i: (0,) * a.ndim)
    ys2 = ys.reshape(2, n, s5w)
    params = (d5, wglu, bglu, wp5, wpl, wo, g1, gf, sh, sc, wr_hi, wr_lo, br)
    return pl.pallas_call(
        functools.partial(_k2_kernel, nb=nb),
        grid=(n // tm,),
        in_specs=[row(dm), row(s5w), pl.BlockSpec((2, tm, s5w), lambda i: (0, i, 0)), yl_spec,
                  row(lw), row(dm), row(dm)] + [full(p) for p in params],
        out_specs=[row(dm), row(dm),
                   pl.BlockSpec((TOP_K, tm), lambda i: (0, i)),
                   pl.BlockSpec((TOP_K, tm), lambda i: (0, i)),
                   pl.BlockSpec((tm // SUB, ne, 1), lambda i: (i, 0, 0))],
        out_shape=[jax.ShapeDtypeStruct((n, dm), F32),
                   jax.ShapeDtypeStruct((n, dm), BF16),
                   jax.ShapeDtypeStruct((TOP_K, n), I32),
                   jax.ShapeDtypeStruct((TOP_K, n), F32),
                   jax.ShapeDtypeStruct((n // SUB, ne, 1), I32)],
        compiler_params=_cparams("arbitrary"),
        name="k2_mix_out",
    )(x2, u, ys2, yl5, gg, m5, ml, *params)


def _chunk_copies(n, cdst_ref, base, local_ref, sorted_ref, sem, to_sorted):
    def per_chunk(c, _):
        g = pl.multiple_of(cdst_ref[base + c], SEG_PAD)
        lo = pl.multiple_of(c * SEG_PAD, SEG_PAD)
        loc = local_ref.at[pl.ds(lo, SEG_PAD)]
        srt = sorted_ref.at[pl.ds(g, SEG_PAD)]
        if to_sorted:
            pltpu.make_async_copy(loc, srt, sem).start()
        else:
            pltpu.make_async_copy(srt, loc, sem).start()
        return 0

    lax.fori_loop(0, n, per_chunk, 0)


def _wait_chunks(n, src_ref, dst_ref, sem):
    def one(c, _):
        pltpu.make_async_copy(src_ref.at[pl.ds(0, SEG_PAD)], dst_ref.at[pl.ds(0, SEG_PAD)],
                              sem).wait()
        return 0

    lax.fori_loop(0, n, one, 0)


def _dispatch_kernel(cdst_ref, nch_ref, tail0_ref, tailn_ref, *refs, ne, part_starts):
    nparts = len(part_starts)
    hn_refs = refs[:nparts]
    ti_ref, locv_ref, xs_ref, slot_ref, xg_ref, zero_ref, sem, tsem = refs[nparts:]
    s = pl.program_id(0)
    ns = pl.num_programs(0)
    buf = s % 2

    @pl.when(s == 0)
    def _():
        zero_ref[...] = jnp.zeros_like(zero_ref)

        def per_e(e, _):
            def per_chunk(c, _):
                d0 = pl.multiple_of(tail0_ref[e] + c * SEG_PAD, SEG_PAD)
                pltpu.make_async_copy(zero_ref, xs_ref.at[pl.ds(d0, SEG_PAD)], tsem).start()
                return 0
            lax.fori_loop(0, tailn_ref[e], per_chunk, 0)
            return 0
        lax.fori_loop(0, ne, per_e, 0)

        def per_e_wait(e, _):
            def per_chunk(c, _):
                pltpu.make_async_copy(zero_ref, xs_ref.at[pl.ds(0, SEG_PAD)], tsem).wait()
                return 0
            lax.fori_loop(0, tailn_ref[e], per_chunk, 0)
            return 0
        lax.fori_loop(0, ne, per_e_wait, 0)

    ti = ti_ref[...].astype(F32)
    eidx = lax.broadcasted_iota(I32, (ne, SUB), 0).astype(F32)
    hit = [eidx == ti[k:k + 1] for k in range(TOP_K)]
    sel = hit[0] | hit[1] | hit[2] | hit[3]
    upper = (lax.broadcasted_iota(I32, (SUB, SUB), 0)
             < lax.broadcasted_iota(I32, (SUB, SUB), 1))
    rank = jnp.dot(jnp.where(sel, 1.0, 0.0).astype(BF16), jnp.where(upper, 1.0, 0.0).astype(BF16),
                   preferred_element_type=F32)
    slot_e = locv_ref[0].astype(F32) + rank
    slots = [jnp.sum(jnp.where(hit[k], slot_e, 0.0), axis=0, keepdims=True) for k in range(TOP_K)]
    slot_ref[...] = jnp.concatenate(slots, axis=0).astype(I32)

    hn = hn_refs[0][...]
    for p in range(1, nparts):
        hn = jnp.where(s >= part_starts[p], hn_refs[p][...], hn)
    blk = 256
    used_rows = nch_ref[s] * SEG_PAD

    def gather_rows(rb):
        sidx = (lax.broadcasted_iota(I32, (blk, SUB), 0) + rb * blk).astype(F32)
        p = (sidx == slots[0]) | (sidx == slots[1]) | (sidx == slots[2]) | (sidx == slots[3])
        xg = jnp.dot(jnp.where(p, 1.0, 0.0).astype(BF16), hn, preferred_element_type=F32)
        xg_ref[buf, rb * blk:(rb + 1) * blk, :] = xg.astype(BF16)

    for rb in range(SLOT_ROWS // blk):
        if (rb + 1) * blk <= SHORT_ROWS:
            gather_rows(rb)
        else:
            pl.when(used_rows > rb * blk)(functools.partial(gather_rows, rb))

    _chunk_copies(nch_ref[s], cdst_ref, s * MAX_CHUNKS, xg_ref.at[buf], xs_ref, sem.at[buf], True)

    @pl.when(s > 0)
    def _():
        _wait_chunks(nch_ref[s - 1], xg_ref.at[1 - buf], xs_ref, sem.at[1 - buf])

    @pl.when(s == ns - 1)
    def _():
        _wait_chunks(nch_ref[s], xg_ref.at[buf], xs_ref, sem.at[buf])


def _dispatch(hn_parts, ti, meta, *, ne):
    dm = hn_parts[0].shape[1]
    counts = [h.shape[0] // SUB for h in hn_parts]
    starts = [sum(counts[:p]) for p in range(len(counts))]
    ns = sum(counts)
    n = ns * SUB

    def part_spec(start, count):
        return pl.BlockSpec((SUB, dm), lambda s, *_: (jnp.clip(s - start, 0, count - 1), 0))

    gs = pltpu.PrefetchScalarGridSpec(
        num_scalar_prefetch=4,
        grid=(ns,),
        in_specs=[part_spec(st, ct) for st, ct in zip(starts, counts)]
                 + [pl.BlockSpec((TOP_K, SUB), lambda s, *_: (0, s)),
                    pl.BlockSpec((1, ne, 1), lambda s, *_: (s, 0, 0))],
        out_specs=[pl.BlockSpec(memory_space=pl.ANY),
                   pl.BlockSpec((TOP_K, SUB), lambda s, *_: (0, s))],
        scratch_shapes=[pltpu.VMEM((2, SLOT_ROWS, dm), BF16),
                        pltpu.VMEM((SEG_PAD, dm), BF16),
                        pltpu.SemaphoreType.DMA((2,)),
                        pltpu.SemaphoreType.DMA(())],
    )
    return pl.pallas_call(
        functools.partial(_dispatch_kernel, ne=ne, part_starts=tuple(starts)),
        grid_spec=gs,
        out_shape=[jax.ShapeDtypeStruct((meta['p_max'], dm), BF16),
                   jax.ShapeDtypeStruct((TOP_K, n), I32)],
        compiler_params=_cparams("arbitrary"),
        name="moe_dispatch",
    )(meta['cdst'], meta['nch'], meta['tail0'], meta['tailn'], *hn_parts, ti, meta['locv'])


def _expert_kernel(te_ref, nt_ref, run_ref, rexp_ref, nrun_ref, x_ref, b1_ref, bd_ref,
                   wu_hbm, wd_hbm, y_ref, wu_buf, wd_buf, w1_ref, w2_ref, act_ref, sem, *, layer):
    i = pl.program_id(0)
    live = i < nt_ref[0]
    run = run_ref[i]
    new_run = (i == 0) | (run_ref[jnp.maximum(i - 1, 0)] != run)
    pair = 2 * LANES
    nblk = wu_buf.shape[-1] // pair

    def weight_copies(expert, slot):
        return (pltpu.make_async_copy(wu_hbm.at[layer, expert], wu_buf.at[slot], sem.at[0, slot]),
                pltpu.make_async_copy(wd_hbm.at[layer, expert], wd_buf.at[slot], sem.at[1, slot]))

    @pl.when(live & new_run)
    def _():
        slot = run % 2

        @pl.when(run == 0)
        def _():
            for cp in weight_copies(rexp_ref[0], 0):
                cp.start()

        for cp in weight_copies(rexp_ref[run], slot):
            cp.wait()

        @pl.when(run + 1 < nrun_ref[0])
        def _():
            for cp in weight_copies(rexp_ref[run + 1], 1 - slot):
                cp.start()

        rr = lax.broadcasted_iota(I32, (pair, pair), 0)
        cc = lax.broadcasted_iota(I32, (pair, pair), 1)
        src = jnp.where(cc < LANES, 2 * cc, 2 * (cc - LANES) + 1)
        perm = jnp.where(rr == src, 1.0, 0.0).astype(BF16)
        for jb in range(nblk):
            cols = slice(jb * pair, (jb + 1) * pair)
            w1_ref[:, cols] = jnp.dot(wu_buf[slot, :, cols].astype(BF16), perm,
                                      preferred_element_type=F32).astype(BF16)
        w2_ref[...] = wd_buf[slot].astype(BF16)

    @pl.when(live)
    def _():
        x = x_ref[...]
        for jb in range(nblk):
            cols = slice(jb * pair, (jb + 1) * pair)
            hb = jnp.dot(x, w1_ref[:, cols], preferred_element_type=F32) + b1_ref[0, :, cols]
            gate = jnp.minimum(hb[:, :LANES], SWIGLU_LIMIT)
            up = jnp.clip(hb[:, LANES:], -SWIGLU_LIMIT, SWIGLU_LIMIT)
            act = gate * _sigmoid(SWIGLU_ALPHA * gate) * (up + 1.0)
            act_ref[:, jb * LANES:(jb + 1) * LANES] = act.astype(BF16)
        y = jnp.dot(act_ref[...], w2_ref[...], preferred_element_type=F32) + bd_ref[0, 0]
        y_ref[...] = y.astype(y_ref.dtype)


def _experts(xs, w_up, b1p, w_down, b_down, meta, layer):
    p_max, dm = xs.shape
    _, ne, _, f2 = w_up.shape
    f = f2 // 2
    n_tiles = p_max // EXPERT_TILE
    tile = lambda i, te, nt, *_: (jnp.minimum(i, jnp.maximum(nt[0] - 1, 0)), 0)
    byexp = lambda i, te, *_: (te[i], 0, 0)
    bylayer = lambda i, te, *_: (layer, te[i], 0, 0)
    gs = pltpu.PrefetchScalarGridSpec(
        num_scalar_prefetch=5,
        grid=(n_tiles,),
        in_specs=[pl.BlockSpec((EXPERT_TILE, dm), tile),
                  pl.BlockSpec((1, 1, f2), byexp), pl.BlockSpec((1, 1, 1, dm), bylayer),
                  pl.BlockSpec(memory_space=pl.ANY), pl.BlockSpec(memory_space=pl.ANY)],
        out_specs=pl.BlockSpec((EXPERT_TILE, dm), tile),
        scratch_shapes=[pltpu.VMEM((2, dm, f2), F32), pltpu.VMEM((2, f, dm), F32),
                        pltpu.VMEM((dm, f2), BF16), pltpu.VMEM((f, dm), BF16),
                        pltpu.VMEM((EXPERT_TILE, f), BF16),
                        pltpu.SemaphoreType.DMA((2, 2))],
    )
    return pl.pallas_call(
        functools.partial(_expert_kernel, layer=layer),
        grid_spec=gs,
        out_shape=jax.ShapeDtypeStruct((p_max, dm), BF16),
        compiler_params=_cparams("arbitrary"),
        name="moe_experts",
    )(meta['tile_e'], meta['n_tiles'], meta['tile_run'], meta['run_e'], meta['n_runs'],
      xs, b1p, b_down, w_up, w_down)


def _combine_kernel(cdst_ref, nch_ref, x_ref, slot_ref, tw_ref, g2_ref, gfin_ref,
                    ys_ref, o_ref, yl_ref, sem, *, s0, nb, final_norm):
    i = pl.program_id(0)
    ni = pl.num_programs(0)
    buf = i % 2

    def fetch(step, b):
        _chunk_copies(nch_ref[s0 + step], cdst_ref, (s0 + step) * MAX_CHUNKS, yl_ref.at[b], ys_ref,
                      sem.at[b], False)

    @pl.when(i == 0)
    def _():
        yl_ref[...] = jnp.zeros_like(yl_ref)
        fetch(0, 0)

    @pl.when(i + 1 < ni)
    def _():
        fetch(i + 1, 1 - buf)

    _wait_chunks(nch_ref[s0 + i], ys_ref, yl_ref.at[buf], sem.at[buf])

    slot = slot_ref[...].astype(F32)
    tw = tw_ref[...]

    def weighted_rows(n_rows):
        sidx = lax.broadcasted_iota(I32, (SUB, n_rows), 1).astype(F32)
        pw = jnp.where(sidx == slot[:, 0:1], tw[:, 0:1], 0.0)
        for k in range(1, TOP_K):
            pw = pw + jnp.where(sidx == slot[:, k:k + 1], tw[:, k:k + 1], 0.0)
        return jnp.dot(pw.astype(BF16), yl_ref[buf, 0:n_rows, :], preferred_element_type=F32)

    f = lax.cond(nch_ref[s0 + i] * SEG_PAD <= SHORT_ROWS,
                 functools.partial(weighted_rows, SHORT_ROWS),
                 functools.partial(weighted_rows, SLOT_ROWS))
    dm = f.shape[-1]
    x2 = x_ref[...].reshape(SUB // nb, nb, dm) + g2_ref[...][None] * f.reshape(SUB // nb, nb, dm)
    x2 = x2.reshape(SUB, dm)
    if final_norm:
        x2 = x2 * lax.rsqrt(jnp.mean(x2 * x2, axis=-1, keepdims=True) + EPS) * gfin_ref[...]
    o_ref[...] = x2


def _combine(x2d, slot_t, tw_t, g2, gfin, ys, meta, *, s0, nb, final_norm):
    n, dm = x2d.shape
    ns = n // SUB
    gs = pltpu.PrefetchScalarGridSpec(
        num_scalar_prefetch=2,
        grid=(ns,),
        in_specs=[pl.BlockSpec((SUB, dm), lambda i, *_: (i, 0)),
                  pl.BlockSpec((SUB, TOP_K), lambda i, *_: (s0 + i, 0)),
                  pl.BlockSpec((SUB, TOP_K), lambda i, *_: (s0 + i, 0)),
                  pl.BlockSpec(g2.shape, lambda i, *_: (0, 0)),
                  pl.BlockSpec(gfin.shape, lambda i, *_: (0, 0)),
                  pl.BlockSpec(memory_space=pl.ANY)],
        out_specs=pl.BlockSpec((SUB, dm), lambda i, *_: (i, 0)),
        scratch_shapes=[pltpu.VMEM((2, SLOT_ROWS, dm), BF16),
                        pltpu.SemaphoreType.DMA((2,))],
    )
    return pl.pallas_call(
        functools.partial(_combine_kernel, s0=s0, nb=nb, final_norm=final_norm),
        grid_spec=gs,
        out_shape=jax.ShapeDtypeStruct((n, dm), F32),
        compiler_params=_cparams("arbitrary"),
        name="moe_combine",
    )(meta['cdst'], meta['nch'], x2d, slot_t, tw_t, g2, gfin, ys)


def _routing_meta(cnt, ne):
    ns = cnt.shape[0]
    pc = (cnt + (SEG_PAD - 1)) // SEG_PAD * SEG_PAD
    loc = jnp.cumsum(pc, axis=1) - pc
    tot = jnp.sum(pc, axis=0)
    reg = (tot + (EXPERT_TILE - 1)) // EXPERT_TILE * EXPERT_TILE
    reg_end = jnp.cumsum(reg)
    base = reg_end - reg
    glob = base[None, :] + jnp.cumsum(pc, axis=0) - pc
    p_max = ns * (SUB * TOP_K + ne * (SEG_PAD - 1)) + ne * (EXPERT_TILE - SEG_PAD)
    p_max = (p_max + EXPERT_TILE - 1) // EXPERT_TILE * EXPERT_TILE
    n_tiles_max = p_max // EXPERT_TILE
    tile_start = jnp.arange(n_tiles_max, dtype=I32) * EXPERT_TILE
    tile_e = jnp.sum((tile_start[:, None] >= reg_end[None, :]).astype(I32), axis=1)
    tile_e = jnp.minimum(tile_e, ne - 1)
    eids = jnp.arange(ne, dtype=I32)
    has = reg > 0
    run_of_e = jnp.cumsum(has.astype(I32)) - 1
    run_e = jnp.sum(jnp.where(has[None, :] & (run_of_e[None, :] == eids[:, None]), eids[None, :], 0),
                    axis=1)
    tile_run = jnp.sum(jnp.where(tile_e[:, None] == eids[None, :], run_of_e[None, :], 0), axis=1)
    crow = jnp.arange(MAX_CHUNKS, dtype=I32) * SEG_PAD
    loc_end = loc + pc
    ce = jnp.sum((crow[None, :, None] >= loc_end[:, None, :]).astype(I32), axis=2)
    ce = jnp.minimum(ce, ne - 1)
    pick = ce[:, :, None] == jnp.arange(ne, dtype=I32)[None, None, :]
    cdst = crow[None, :] + jnp.sum(jnp.where(pick, (glob - loc)[:, None, :], 0), axis=2)
    return dict(cdst=cdst.reshape(-1).astype(I32),
                nch=(jnp.sum(pc, axis=1) // SEG_PAD).astype(I32),
                tail0=(base + tot).astype(I32), tailn=((reg - tot) // SEG_PAD).astype(I32),
                locv=loc.reshape(ns, ne, 1).astype(I32),
                tile_e=tile_e, n_tiles=(reg_end[-1:] // EXPERT_TILE).astype(I32), p_max=p_max,
                tile_run=jnp.maximum(tile_run, 0).astype(I32), run_e=run_e.astype(I32),
                n_runs=jnp.sum(has.astype(I32))[None])


def kernel(x, c, ctx, c_ctx, w_ada, b_ada, g_mix, g_ffn, w_in, s5_lam_re, s5_lam_im, s5_log_dt, s5_b_re, s5_b_im, s5_c_re, s5_c_im, s5_d, s5_w_glu, s5_b_glu, lru_conv_w, lru_conv_b, lru_lam, lru_w_a, lru_b_a, lru_w_x, lru_b_x, w_proj_s5, w_proj_lru, w_out, w_router, b_router, w_up, b_up, w_down, b_down, g_final):
    nb, seq, dm = x.shape
    tc = ctx.shape[1]
    depth = w_ada.shape[0]
    r = seq // GRID_W
    s5w = s5_d.shape[-1]
    lw = lru_conv_w.shape[-1]
    ne = w_router.shape[-1]
    nl, ncx = seq * nb, tc * nb

    xl = jnp.transpose(x, (1, 0, 2)).reshape(nl, dm)
    xc = jnp.transpose(ctx, (1, 0, 2)).reshape(ncx, dm)

    pad_rows = (-(nb + 1)) % 8
    cvec = jnp.concatenate([c, c_ctx[None, :], jnp.zeros((pad_rows, dm), F32)], axis=0)
    ada = _ada_all(cvec, w_ada, b_ada)

    s5_ops = jax.vmap(_s5_operators)(s5_lam_re, s5_lam_im, s5_log_dt, s5_b_re, s5_b_im,
                                     s5_c_re, s5_c_im)
    lam = lru_lam.astype(F32)
    softplus = jnp.maximum(-lam, 0.0) + jnp.log1p(jnp.exp(-jnp.abs(lam)))
    sp_all = (-0.5 * LRU_C) * softplus[:, :, None, :]
    wg_all = jax.vmap(_lru_gate_weights)(0.5 * lru_w_a, 0.5 * lru_w_x)
    b_a_all = 0.5 * lru_b_a[:, :, None, :]
    b_x_all = 0.5 * lru_b_x[:, :, None, :]

    for l in range(depth):
        need_ctx = l < depth - 1
        mods_l =[ada[l, :nb, k * dm:(k + 1) * dm] for k in range(6)]
        mods_c = [jnp.broadcast_to(ada[l, nb:nb + 1, k * dm:(k + 1) * dm], (nb, dm)) for k in range(6)]
        gm = g_mix[l][None, :]
        gf = g_ffn[l][None, :]
        w_in_l = w_in[l].astype(BF16)

        u_l, xr_l, gg_l, m5_l, ml_l = _k1(xl, gm, mods_l[0], mods_l[1], w_in_l, nb=nb, r=r,
                                          colmajor=True, s5w=s5w, lw=lw, need_gates=True)
        outs_c = _k1(xc, gm, mods_c[0], mods_c[1], w_in_l, nb=nb, r=r, colmajor=False,
                     s5w=s5w, lw=lw, need_gates=need_ctx)
        u_c, xr_c = outs_c[0], outs_c[1]

        bm, cm, a_re, a_im = [a[l] for a in s5_ops]
        nslab = s5w // LANES
        h0 = jnp.zeros((2, nslab, nb, 2 * S5_SLAB_STATE), F32)
        ys_c, hs_c = _s5_scan(u_c.reshape(tc, nb, s5w), bm, cm, a_re, a_im, h0)
        ys_l, _ = _s5_scan(u_l.reshape(seq, nb, s5w), bm, cm, a_re, a_im, hs_c)

        sp, wg, b_a, b_x = sp_all[l], wg_all[l], b_a_all[l], b_x_all[l]
        cw = lru_conv_w[l]
        cbias = lru_conv_b[l][None, :]
        hz = jnp.zeros((2, nb, lw), F32)
        cb_c = 2 if (tc // r) % 2 == 0 else 1
        cb_l = 2 if GRID_W % 2 == 0 else 1
        yl_c, hl_c = _lru_scan(xr_c, cw, cbias, sp, wg, b_a, b_x, hz, cb=cb_c)
        yl_l, _ = _lru_scan(xr_l, cw, cbias, sp, wg, b_a, b_x, hl_c, cb=cb_l)

        k2_params = (s5_d[l][None, :], s5_w_glu[l].astype(BF16), s5_b_glu[l][None, :],
                     w_proj_s5[l].astype(BF16), w_proj_lru[l].astype(BF16), w_out[l].astype(BF16))
        wr_t = jnp.transpose(w_router[l])
        wr_hi = wr_t.astype(BF16)
        wr_lo = (wr_t - wr_hi.astype(F32)).astype(BF16)
        br = b_router[l][:, None]
        x1_l, hn_l, ti_l, tw_l, cnt_l = _k2(xl, u_l, ys_l, yl_l, gg_l, m5_l, ml_l, *k2_params,
                                            mods_l[2], gf, mods_l[3], mods_l[4], wr_hi, wr_lo, br,
                                            nb=nb, r=r, colmajor=True)
        if need_ctx:
            _, _, gg_c, m5_c, ml_c = outs_c
            x1_c, hn_c, ti_c, tw_c, cnt_c = _k2(xc, u_c, ys_c, yl_c, gg_c, m5_c, ml_c, *k2_params,
                                                mods_c[2], gf, mods_c[3], mods_c[4], wr_hi, wr_lo, br,
                                                nb=nb, r=r, colmajor=False)
            hn_parts = [hn_c, hn_l]
            ti = jnp.concatenate([ti_c, ti_l], axis=1)
            tw = jnp.concatenate([tw_c, tw_l], axis=1)
            cnt = jnp.concatenate([cnt_c, cnt_l], axis=0)
        else:
            hn_parts, ti, tw, cnt = [hn_l], ti_l, tw_l, cnt_l

        meta = _routing_meta(cnt[:, :, 0], ne)
        xs, slots = _dispatch(hn_parts, ti, meta, ne=ne)
        f2 = w_up.shape[-1]
        b1p = jnp.transpose(b_up[l].reshape(ne, f2 // (2 * LANES), LANES, 2), (0, 1, 3, 2))
        ys = _experts(xs, w_up, b1p.reshape(ne, 1, f2), w_down, b_down[:, :, None, :], meta, l)
        slot_t = jnp.transpose(slots)
        tw_t = jnp.transpose(tw)
        last = l == depth - 1
        gfin = g_final[None, :]
        if need_ctx:
            xc = _combine(x1_c, slot_t, tw_t, mods_c[5], gfin, ys, meta, s0=0, nb=nb,
                          final_norm=False)
            xl = _combine(x1_l, slot_t, tw_t, mods_l[5], gfin, ys, meta, s0=ncx // SUB,
                          nb=nb, final_norm=False)
        else:
            xl = _combine(x1_l, slot_t, tw_t, mods_l[5], gfin, ys, meta, s0=0, nb=nb,
                          final_norm=last)

    return jnp.transpose(xl.reshape(seq, nb, dm), (1, 0, 2)).astype(x.dtype)
```

```python
import functools
import math

import jax
import jax.numpy as jnp
from jax import lax
from jax.experimental import pallas as pl
from jax.experimental.pallas import tpu as pltpu

F32 = jnp.float32
BF16 = jnp.bfloat16
I32 = jnp.int32
HIGHEST = lax.Precision.HIGHEST

GRID_W = 64
S5_GROUP = 16
S5_STATE = 64
S5_DT_MIN = 0.001
S5_DT_MAX = 0.1
LRU_BLOCK = 64
LRU_C = 8.0
CONV_W = 4
TOP_K = 4
SWIGLU_LIMIT = 7.0
SWIGLU_ALPHA = 1.702
EPS = 1e-6

LANES = 128
BF16_ROWS = 16
VMEM_LIMIT = 56 * 1024 * 1024

ROW_TILE = 1024
K2_TILE = 512
S5_SLAB_GROUPS = LANES // S5_GROUP
S5_SLAB_STATE = S5_SLAB_GROUPS * S5_STATE
SUB = 256
SEG_PAD = BF16_ROWS
SLOT_ROWS = 1536
MAX_CHUNKS = SLOT_ROWS // SEG_PAD
MAX_PAIRS = MAX_CHUNKS // 2
SHORT_ROWS = SUB * TOP_K + 256
EXPERT_TILE = 1024


def _cparams(*sem, **kw):
    return pltpu.CompilerParams(dimension_semantics=sem, vmem_limit_bytes=VMEM_LIMIT, **kw)


def _gelu(x):
    c0 = math.sqrt(2.0 / math.pi)
    half = 0.5 * x
    return half + half * jnp.tanh(x * (c0 + (c0 * 0.044715) * (x * x)))


def _sigmoid(x):
    return 0.5 * jnp.tanh(0.5 * x) + 0.5


def _rms_mod(x, g, sh, sc, nb):
    tm, d = x.shape
    y = x * lax.rsqrt(jnp.mean(x * x, axis=-1, keepdims=True) + EPS) * g
    y3 = y.reshape(tm // nb, nb, d) * (1.0 + sc)[None] + sh[None]
    return y3.reshape(tm, d)


def _ada_kernel(c_ref, w_ref, b_ref, o_ref):
    c = c_ref[...]
    cond = c * _sigmoid(c)
    o_ref[0] = jnp.dot(cond, w_ref[0], preferred_element_type=F32, precision=HIGHEST) + b_ref[0]


def _ada_all(cvec, w_ada, b_ada):
    nl, d, w6 = w_ada.shape
    r = cvec.shape[0]
    tn = min(w6, 1536)
    return pl.pallas_call(
        _ada_kernel,
        grid=(nl, w6 // tn),
        in_specs=[pl.BlockSpec((r, d), lambda l, j: (0, 0)),
                  pl.BlockSpec((1, d, tn), lambda l, j: (l, 0, j)),
                  pl.BlockSpec((1, 1, tn), lambda l, j: (l, 0, j))],
        out_specs=pl.BlockSpec((1, r, tn), lambda l, j: (l, 0, j)),
        out_shape=jax.ShapeDtypeStruct((nl, r, w6), F32),
        compiler_params=_cparams("arbitrary", "arbitrary"),
        name="ada",
    )(cvec, w_ada, b_ada.reshape(nl, 1, w6))


def _k1(x2, g, sh, sc, w_in, *, nb, r, colmajor, s5w, lw, need_gates):
    n, d = x2.shape
    t = n // nb
    ncol = t // r
    if colmajor:
        tm = (t // r) * nb
        xr_spec = pl.BlockSpec((ncol, 1, nb, lw), lambda i: (0, i, 0, 0))
    else:
        cbk = max(1, min(ncol, ROW_TILE // (r * nb)))
        tm = cbk * r * nb
        xr_spec = pl.BlockSpec((cbk, r, nb, lw), lambda i: (i, 0, 0, 0))
    splits = (s5w, s5w + lw, s5w + 2 * lw, s5w + 2 * lw + d, s5w + 2 * lw + 2 * d)
    if not need_gates:
        splits = splits[:2]
    row = lambda w: pl.BlockSpec((tm, w), lambda i: (i, 0))
    full = lambda a: pl.BlockSpec(a.shape, lambda i: (0,) * a.ndim)
    out_specs = [row(s5w), xr_spec, row(lw), row(d), row(d)]
    out_shape = [jax.ShapeDtypeStruct((n, s5w), F32),
                 jax.ShapeDtypeStruct((ncol, r, nb, lw), F32),
                 jax.ShapeDtypeStruct((n, lw), BF16),
                 jax.ShapeDtypeStruct((n, d), BF16),
                 jax.ShapeDtypeStruct((n, d), BF16)]
    k = len(splits)

    def body(x_ref, g_ref, sh_ref, sc_ref, w_ref, *o_refs):
        hn = _rms_mod(x_ref[...], g_ref[...], sh_ref[...], sc_ref[...], nb).astype(BF16)
        lo = 0
        for o_ref, hi in zip(o_refs, splits):
            v = jnp.dot(hn, w_ref[:, lo:hi], preferred_element_type=F32).astype(o_ref.dtype)
            o_ref[...] = v.reshape(o_ref.shape)
            lo = hi

    w_used = w_in[:, :splits[-1]]
    return pl.pallas_call(
        body,
        grid=(n // tm,),
        in_specs=[row(d), full(g), full(sh), full(sc), full(w_used)],
        out_specs=out_specs[:k],
        out_shape=out_shape[:k],
        compiler_params=_cparams("arbitrary"),
        name="k1_in_proj",
    )(x2, g, sh, sc, w_used)


def _s5_kernel(u_ref, bm_ref, cm_ref, are_ref, aim_ref, h0_ref, y_ref, ht_ref, hs_ref, st_ref,
               *, tt, nb, nslab):
    d = pl.program_id(0)
    j = pl.program_id(1)
    nj = pl.num_programs(1)
    sw = S5_SLAB_STATE
    rows = tt * nb

    @pl.when(j == 0)
    def _():
        st_ref[...] = h0_ref[0]

    def project_in(m):
        ub = u_ref[:, :, m * LANES:(m + 1) * LANES].reshape(rows, LANES).astype(BF16)
        hs_ref[m] = jnp.dot(ub, bm_ref[0, m], preferred_element_type=F32)

    def recur(m, reverse):
        a_re = jnp.broadcast_to(are_ref[0, m], (nb, sw))
        a_im = jnp.broadcast_to(aim_ref[0, m], (nb, sw))
        hr = st_ref[m, :, :sw]
        hi = st_ref[m, :, sw:]
        for i in range(tt):
            t = tt - 1 - i if reverse else i
            rs = slice(t * nb, (t + 1) * nb)
            nr = a_re * hr - a_im * hi + hs_ref[m, rs, :sw]
            ni = a_re * hi + a_im * hr + hs_ref[m, rs, sw:]
            hs_ref[m, rs, :sw] = nr
            hs_ref[m, rs, sw:] = ni
            hr, hi = nr, ni
        st_ref[m, :, :sw] = hr
        st_ref[m, :, sw:] = hi

    def project_out(m):
        y = jnp.dot(hs_ref[m].astype(BF16), cm_ref[0, m], preferred_element_type=F32)
        y_ref[0, :, :, m * LANES:(m + 1) * LANES] = y.reshape(tt, nb, LANES)

    def run(reverse):
        project_in(0)
        for m in range(nslab):
            if m + 1 < nslab:
                project_in(m + 1)
            recur(m, reverse)
            project_out(m)

    @pl.when(d == 0)
    def _():
        run(False)

    @pl.when(d == 1)
    def _():
        run(True)

    @pl.when(j == nj - 1)
    def _():
        ht_ref[0] = st_ref[...]


def _s5_scan(u3, bm, cm, a_re, a_im, h0):
    t, nb, s5w = u3.shape
    nslab = s5w // LANES
    tt = 64 if t % 64 == 0 else t
    nj = t // tt
    sw2 = 2 * S5_SLAB_STATE
    blk = lambda d, j: j + d * (nj - 1 - 2 * j)
    return pl.pallas_call(
        functools.partial(_s5_kernel, tt=tt, nb=nb, nslab=nslab),
        grid=(2, nj),
        in_specs=[pl.BlockSpec((tt, nb, s5w), lambda d, j: (blk(d, j), 0, 0)),
                  pl.BlockSpec((1, nslab, LANES, sw2), lambda d, j: (d, 0, 0, 0)),
                  pl.BlockSpec((1, nslab, sw2, LANES), lambda d, j: (d, 0, 0, 0)),
                  pl.BlockSpec((1, nslab, 1, S5_SLAB_STATE), lambda d, j: (d, 0, 0, 0)),
                  pl.BlockSpec((1, nslab, 1, S5_SLAB_STATE), lambda d, j: (d, 0, 0, 0)),
                  pl.BlockSpec((1, nslab, nb, sw2), lambda d, j: (d, 0, 0, 0))],
        out_specs=[pl.BlockSpec((1, tt, nb, s5w), lambda d, j: (d, blk(d, j), 0, 0)),
                   pl.BlockSpec((1, nslab, nb, sw2), lambda d, j: (d, 0, 0, 0))],
        out_shape=[jax.ShapeDtypeStruct((2, t, nb, s5w), F32),
                   jax.ShapeDtypeStruct((2, nslab, nb, sw2), F32)],
        scratch_shapes=[pltpu.VMEM((nslab, tt * nb, sw2), F32),
                        pltpu.VMEM((nslab, nb, sw2), F32)],
        compiler_params=_cparams("arbitrary", "arbitrary"),
        name="s5_scan",
    )(u3, bm, cm, a_re, a_im, h0)


def _s5_operators(lam_re, lam_im, log_dt, b_re, b_im, c_re, c_im):
    lr, li = lam_re.astype(F32), lam_im.astype(F32)
    dt = jnp.exp(log_dt.astype(F32))[..., None]
    mag = jnp.exp(lr * dt)
    ab_re, ab_im = mag * jnp.cos(li * dt), mag * jnp.sin(li * dt)
    den = lr * lr + li * li
    zr = ab_re - 1.0
    q_re = (zr * lr + ab_im * li) / den
    q_im = (ab_im * lr - zr * li) / den
    br, bi = b_re.astype(F32), b_im.astype(F32)
    bb_re = q_re[..., None] * br - q_im[..., None] * bi
    bb_im = q_re[..., None] * bi + q_im[..., None] * br
    two, g, p, h = bb_re.shape
    gs = S5_SLAB_GROUPS
    nslab = g // gs
    eye = jnp.eye(gs, dtype=F32)

    def in_mat(bb):
        bb = bb.reshape(two, nslab, gs, p, h)
        m = jnp.einsum('dmgph,gk->dmghkp', bb, eye)
        return m.reshape(two, nslab, gs * h, gs * p)

    def out_mat(cc):
        cc = cc.astype(F32).reshape(two, nslab, gs, h, p)
        m = jnp.einsum('dmghp,gk->dmgpkh', cc, eye)
        return m.reshape(two, nslab, gs * p, gs * h)

    bm = jnp.concatenate([in_mat(bb_re), in_mat(bb_im)], axis=-1).astype(BF16)
    cm = jnp.concatenate([out_mat(c_re), -out_mat(c_im)], axis=-2).astype(BF16)
    a_re = ab_re.reshape(two, nslab, 1, gs * p)
    a_im = ab_im.reshape(two, nslab, 1, gs * p)
    return bm, cm, a_re, a_im


def _lru_kernel(x_ref, xp_ref, xn_ref, cw_ref, cb_ref, c1_ref, wg_ref, ba_ref, bx_ref, h0_ref,
                y_ref, ht_ref, xpad_ref, xc_ref, a_ref, b_ref, st_ref, *, cb, r, nb):
    d = pl.program_id(0)
    j = pl.program_id(1)
    nj = pl.num_programs(1)
    jeff = j + d * (nj - 1 - 2 * j)
    steps = cb * r
    rows = steps * nb
    w = x_ref.shape[-1]

    @pl.when(j == 0)
    def _():
        st_ref[...] = h0_ref[0]

    has_prev = (jeff > 0).astype(F32)
    has_next = (jeff < nj - 1).astype(F32)
    xpad_ref[0:nb, :] = xp_ref[0, 0].astype(F32) * has_prev
    xpad_ref[nb:nb + rows, :] = x_ref[...].reshape(rows, w).astype(F32)
    xpad_ref[nb + rows:3 * nb + rows, :] = xn_ref[0].reshape(2 * nb, w).astype(F32) * has_next
    acc = cb_ref[...] + cw_ref[0:1, :] * xpad_ref[0:rows, :]
    for k in range(1, CONV_W):
        acc = acc + cw_ref[k:k + 1, :] * xpad_ref[k * nb:k * nb + rows, :]
    xc_ref[...] = acc

    for s in range(w // LANES):
        ls = slice(s * LANES, (s + 1) * LANES)
        xs = xc_ref[:, ls]
        gs = jnp.dot(xs.astype(BF16), wg_ref[0, s], preferred_element_type=F32)
        tr = jnp.tanh(gs[:, :LANES] + ba_ref[0, :, ls])
        ig = 0.5 * jnp.tanh(gs[:, LANES:] + bx_ref[0, :, ls]) + 0.5
        c1 = c1_ref[0, :, ls]
        a = jnp.exp(c1 * tr + c1)
        a_ref[:, ls] = a
        s1 = 1.0 - a * a
        root = jnp.where(s1 > 0.0, s1 * lax.rsqrt(s1), 0.0)
        b_ref[:, ls] = root * (ig * xs)

    def step(i, h):
        t = jnp.where(d == 0, i, steps - 1 - i)
        r0 = pl.multiple_of(t * nb, nb)
        h = a_ref[pl.ds(r0, nb), :] * h + b_ref[pl.ds(r0, nb), :]
        y_ref[0, t // r, t % r] = h
        return h

    st_ref[...] = lax.fori_loop(0, steps, step, st_ref[...], unroll=4)

    @pl.when(j == nj - 1)
    def _():
        ht_ref[0] = st_ref[...]


def _lru_scan(x4, conv_w, conv_b, sp, wg, b_a, b_x, h0, *, cb):
    c, r, nb, w = x4.shape
    nj = c // cb
    rows = cb * r * nb
    blk = lambda d, j: j + d * (nj - 1 - 2 * j)
    full2 = lambda a: pl.BlockSpec(a.shape, lambda d, j: (0,) * a.ndim)
    perdir = lambda a: pl.BlockSpec((1,) + a.shape[1:], lambda d, j: (d,) + (0,) * (a.ndim - 1))
    return pl.pallas_call(
        functools.partial(_lru_kernel, cb=cb, r=r, nb=nb),
        grid=(2, nj),
        in_specs=[pl.BlockSpec((cb, r, nb, w), lambda d, j: (blk(d, j), 0, 0, 0)),
                  pl.BlockSpec((1, 1, nb, w),
                               lambda d, j: (jnp.maximum(blk(d, j) * cb - 1, 0), r - 1, 0, 0)),
                  pl.BlockSpec((1, 2, nb, w),
                               lambda d, j: (jnp.minimum((blk(d, j) + 1) * cb, c - 1), 0, 0, 0)),
                  full2(conv_w), full2(conv_b), perdir(sp), perdir(wg), perdir(b_a), perdir(b_x),
                  perdir(h0)],
        out_specs=[pl.BlockSpec((1, cb, r, nb, w), lambda d, j: (d, blk(d, j), 0, 0, 0)),
                   pl.BlockSpec((1, nb, w), lambda d, j: (d, 0, 0))],
        out_shape=[jax.ShapeDtypeStruct((2, c, r, nb, w), F32),
                   jax.ShapeDtypeStruct((2, nb, w), F32)],
        scratch_shapes=[pltpu.VMEM((rows + 3 * nb, w), F32),
                        pltpu.VMEM((rows, w), F32),
                        pltpu.VMEM((rows, w), F32),
                        pltpu.VMEM((rows, w), F32),
                        pltpu.VMEM((nb, w), F32)],
        compiler_params=_cparams("arbitrary", "arbitrary"),
        name="lru_scan",
    )(x4, x4, x4, conv_w, conv_b, sp, wg, b_a, b_x, h0)


def _lru_gate_weights(w_a, w_x):
    two, nblk, k, _ = w_a.shape
    per = LANES // k
    eye = jnp.eye(per, dtype=F32)

    def slabs(wm):
        wm = wm.astype(F32).reshape(two, nblk // per, per, k, k)
        m = jnp.einsum('dsgij,gk->dsgikj', wm, eye)
        return m.reshape(two, nblk // per, LANES, LANES)

    return jnp.concatenate([slabs(w_a), slabs(w_x)], axis=-1).astype(BF16)


def _k2_kernel(x_ref, u_ref, ys_ref, yl_ref, gg_ref, m5_ref, ml_ref, d5_ref, wglu_ref, bglu_ref,
               wp5_ref, wpl_ref, wo_ref, g1_ref, gf_ref, sh_ref, sc_ref, wrh_ref, wrl_ref, br_ref,
               x1_ref, hn_ref, ti_ref, tw_ref, cnt_ref, *, nb):
    tm, dm = x_ref.shape
    u = u_ref[...]
    y5 = ys_ref[0] + ys_ref[1] + d5_ref[...] * u
    gy = _gelu(y5)
    o5 = gy * _sigmoid(jnp.dot(gy.astype(BF16), wglu_ref[...], preferred_element_type=F32)
                       + bglu_ref[...])
    yl = (yl_ref[0] + yl_ref[1]).reshape(tm, -1)
    ol = _gelu(gg_ref[...].astype(F32)) * yl
    merged = (_sigmoid(m5_ref[...]).astype(F32)
              * jnp.dot(o5.astype(BF16), wp5_ref[...], preferred_element_type=F32)
              + _sigmoid(ml_ref[...]).astype(F32)
              * jnp.dot(ol.astype(BF16), wpl_ref[...], preferred_element_type=F32))
    mix = jnp.dot(merged.astype(BF16), wo_ref[...], preferred_element_type=F32)
    x1 = (x_ref[...].reshape(tm // nb, nb, dm) + g1_ref[...][None] * mix.reshape(tm // nb, nb, dm))
    x1 = x1.reshape(tm, dm)
    x1_ref[...] = x1
    hn = _rms_mod(x1, gf_ref[...], sh_ref[...], sc_ref[...], nb)
    hn_hi = hn.astype(BF16)
    hn_ref[...] = hn_hi
    hn_lo = (hn - hn_hi.astype(F32)).astype(BF16)

    nt = (((1,), (1,)), ((), ()))
    logits = (lax.dot_general(wrh_ref[...], hn_hi, nt, preferred_element_type=F32)
              + lax.dot_general(wrl_ref[...], hn_hi, nt, preferred_element_type=F32)
              + lax.dot_general(wrh_ref[...], hn_lo, nt, preferred_element_type=F32)) + br_ref[...]
    ne = logits.shape[0]
    eidx = lax.broadcasted_iota(I32, (ne, tm), 0).astype(F32)
    vals = logits
    tv, ti = [], []
    for _ in range(TOP_K):
        m = jnp.max(vals, axis=0, keepdims=True)
        idx = jnp.min(jnp.where(vals == m, eidx, float(ne)), axis=0, keepdims=True)
        tv.append(m)
        ti.append(idx)
        vals = jnp.where(eidx == idx, -jnp.inf, vals)
    ex = [jnp.exp(v - tv[0]) for v in tv]
    den = ex[0] + ex[1] + ex[2] + ex[3]
    ti_ref[...] = jnp.concatenate(ti, axis=0).astype(I32)
    tw_ref[...] = jnp.concatenate([e / den for e in ex], axis=0)
    sel = (eidx == ti[0]) | (eidx == ti[1]) | (eidx == ti[2]) | (eidx == ti[3])
    self32 = jnp.where(sel, 1.0, 0.0)
    for sb in range(tm // SUB):
        cnt_ref[sb] = jnp.sum(self32[:, sb * SUB:(sb + 1) * SUB], axis=1, keepdims=True).astype(I32)


def _k2(x2, u, ys, yl5, gg, m5, ml, d5, wglu, bglu, wp5, wpl, wo, g1, gf, sh, sc, wr_hi, wr_lo, br,
        *, nb, r, colmajor):
    n, dm = x2.shape
    t = n // nb
    ncol = t // r
    lw = yl5.shape[-1]
    s5w = u.shape[-1]
    ne = wr_hi.shape[0]
    if colmajor:
        cbk = max(1, min(ncol, K2_TILE // nb))
        per_row = ncol // cbk
        tm = cbk * nb
        yl_spec = pl.BlockSpec((2, cbk, 1, nb, lw),
                               lambda i: (0, i % per_row, i // per_row, 0, 0))
    else:
        cbk = max(1, min(ncol, K2_TILE // (r * nb)))
        tm = cbk * r * nb
        yl_spec = pl.BlockSpec((2, cbk, r, nb, lw), lambda i: (0, i, 0, 0, 0))
    row = lambda w: pl.BlockSpec((tm, w), lambda i: (i, 0))
    full = lambda a: pl.BlockSpec(a.shape, lambda i: (0,) * a.ndim)
    ys2 = ys.reshape(2, n, s5w)
    params = (d5, wglu, bglu, wp5, wpl, wo, g1, gf, sh, sc, wr_hi, wr_lo, br)
    return pl.pallas_call(
        functools.partial(_k2_kernel, nb=nb),
        grid=(n // tm,),
        in_specs=[row(dm), row(s5w), pl.BlockSpec((2, tm, s5w), lambda i: (0, i, 0)), yl_spec,
                  row(lw), row(dm), row(dm)] + [full(p) for p in params],
        out_specs=[row(dm), row(dm),
                   pl.BlockSpec((TOP_K, tm), lambda i: (0, i)),
                   pl.BlockSpec((TOP_K, tm), lambda i: (0, i)),
                   pl.BlockSpec((tm // SUB, ne, 1), lambda i: (i, 0, 0))],
        out_shape=[jax.ShapeDtypeStruct((n, dm), F32),
                   jax.ShapeDtypeStruct((n, dm), BF16),
                   jax.ShapeDtypeStruct((TOP_K, n), I32),
                   jax.ShapeDtypeStruct((TOP_K, n), F32),
                   jax.ShapeDtypeStruct((n // SUB, ne, 1), I32)],
        compiler_params=_cparams("arbitrary"),
        name="k2_mix_out",
    )(x2, u, ys2, yl5, gg, m5, ml, *params)


def _start_copies(n, rows, lrow_ref, grow_ref, base, local_ref, sorted_ref, sem, to_sorted):
    def one(k, _):
        lo = pl.multiple_of(lrow_ref[base + k], SEG_PAD)
        g = pl.multiple_of(grow_ref[base + k], SEG_PAD)
        loc = local_ref.at[pl.ds(lo, rows)]
        srt = sorted_ref.at[pl.ds(g, rows)]
        if to_sorted:
            pltpu.make_async_copy(loc, srt, sem).start()
        else:
            pltpu.make_async_copy(srt, loc, sem).start()
        return 0

    lax.fori_loop(0, n, one, 0)


def _wait_copies(n, rows, src_ref, dst_ref, sem):
    def one(k, _):
        pltpu.make_async_copy(src_ref.at[pl.ds(0, rows)], dst_ref.at[pl.ds(0, rows)], sem).wait()
        return 0

    lax.fori_loop(0, n, one, 0)


class _CopyLists:
    def __init__(self, refs):
        self.pl_ref, self.pg_ref, self.pn_ref, self.sl_ref, self.sg_ref, self.sn_ref = refs

    def start(self, s, local_ref, sorted_ref, sems, to_sorted):
        _start_copies(self.pn_ref[s], 2 * SEG_PAD, self.pl_ref, self.pg_ref, s * MAX_PAIRS,
                      local_ref, sorted_ref, sems.at[0], to_sorted)
        _start_copies(self.sn_ref[s], SEG_PAD, self.sl_ref, self.sg_ref, s * MAX_PAIRS,
                      local_ref, sorted_ref, sems.at[1], to_sorted)

    def wait(self, s, src_ref, dst_ref, sems):
        _wait_copies(self.pn_ref[s], 2 * SEG_PAD, src_ref, dst_ref, sems.at[0])
        _wait_copies(self.sn_ref[s], SEG_PAD, src_ref, dst_ref, sems.at[1])


def _dispatch_kernel(*refs, ne, part_starts):
    lists = _CopyLists(refs[:6])
    nch_ref, tail0_ref, tailn_ref = refs[6:9]
    nparts = len(part_starts)
    hn_refs = refs[9:9 + nparts]
    ti_ref, locv_ref, xs_ref, slot_ref, xg_ref, zero_ref, sem, tsem = refs[9 + nparts:]
    s = pl.program_id(0)
    ns = pl.num_programs(0)
    buf = s % 2

    @pl.when(s == 0)
    def _():
        zero_ref[...] = jnp.zeros_like(zero_ref)

        def per_e(e, _):
            def per_chunk(c, _):
                d0 = pl.multiple_of(tail0_ref[e] + c * SEG_PAD, SEG_PAD)
                pltpu.make_async_copy(zero_ref, xs_ref.at[pl.ds(d0, SEG_PAD)], tsem).start()
                return 0
            lax.fori_loop(0, tailn_ref[e], per_chunk, 0)
            return 0
        lax.fori_loop(0, ne, per_e, 0)

        def per_e_wait(e, _):
            def per_chunk(c, _):
                pltpu.make_async_copy(zero_ref, xs_ref.at[pl.ds(0, SEG_PAD)], tsem).wait()
                return 0
            lax.fori_loop(0, tailn_ref[e], per_chunk, 0)
            return 0
        lax.fori_loop(0, ne, per_e_wait, 0)

    ti = ti_ref[...].astype(F32)
    eidx = lax.broadcasted_iota(I32, (ne, SUB), 0).astype(F32)
    hit = [eidx == ti[k:k + 1] for k in range(TOP_K)]
    sel = hit[0] | hit[1] | hit[2] | hit[3]
    upper = (lax.broadcasted_iota(I32, (SUB, SUB), 0)
             < lax.broadcasted_iota(I32, (SUB, SUB), 1))
    rank = jnp.dot(jnp.where(sel, 1.0, 0.0).astype(BF16), jnp.where(upper, 1.0, 0.0).astype(BF16),
                   preferred_element_type=F32)
    slot_e = locv_ref[0].astype(F32) + rank
    slots = [jnp.sum(jnp.where(hit[k], slot_e, 0.0), axis=0, keepdims=True) for k in range(TOP_K)]
    slot_ref[...] = jnp.concatenate(slots, axis=0).astype(I32)

    hn = hn_refs[0][...]
    for p in range(1, nparts):
        hn = jnp.where(s >= part_starts[p], hn_refs[p][...], hn)
    blk = 256
    used_rows = nch_ref[s] * SEG_PAD

    def gather_rows(rb):
        sidx = (lax.broadcasted_iota(I32, (blk, SUB), 0) + rb * blk).astype(F32)
        p = (sidx == slots[0]) | (sidx == slots[1]) | (sidx == slots[2]) | (sidx == slots[3])
        xg = jnp.dot(jnp.where(p, 1.0, 0.0).astype(BF16), hn, preferred_element_type=F32)
        xg_ref[buf, rb * blk:(rb + 1) * blk, :] = xg.astype(BF16)

    for rb in range(SLOT_ROWS // blk):
        if (rb + 1) * blk <= SHORT_ROWS:
            gather_rows(rb)
        else:
            pl.when(used_rows > rb * blk)(functools.partial(gather_rows, rb))

    lists.start(s, xg_ref.at[buf], xs_ref, sem.at[buf], True)

    @pl.when(s > 0)
    def _():
        lists.wait(s - 1, xg_ref.at[1 - buf], xs_ref, sem.at[1 - buf])

    @pl.when(s == ns - 1)
    def _():
        lists.wait(s, xg_ref.at[buf], xs_ref, sem.at[buf])


def _dispatch(hn_parts, ti, meta, *, ne):
    dm = hn_parts[0].shape[1]
    counts = [h.shape[0] // SUB for h in hn_parts]
    starts = [sum(counts[:p]) for p in range(len(counts))]
    ns = sum(counts)
    n = ns * SUB

    def part_spec(start, count):
        return pl.BlockSpec((SUB, dm), lambda s, *_: (jnp.clip(s - start, 0, count - 1), 0))

    assert ne <= MAX_PAIRS
    gs = pltpu.PrefetchScalarGridSpec(
        num_scalar_prefetch=9,
        grid=(ns,),
        in_specs=[part_spec(st, ct) for st, ct in zip(starts, counts)]
                 + [pl.BlockSpec((TOP_K, SUB), lambda s, *_: (0, s)),
                    pl.BlockSpec((1, ne, 1), lambda s, *_: (s, 0, 0))],
        out_specs=[pl.BlockSpec(memory_space=pl.ANY),
                   pl.BlockSpec((TOP_K, SUB), lambda s, *_: (0, s))],
        scratch_shapes=[pltpu.VMEM((2, SLOT_ROWS, dm), BF16),
                        pltpu.VMEM((SEG_PAD, dm), BF16),
                        pltpu.SemaphoreType.DMA((2, 2)),
                        pltpu.SemaphoreType.DMA(())],
    )
    return pl.pallas_call(
        functools.partial(_dispatch_kernel, ne=ne, part_starts=tuple(starts)),
        grid_spec=gs,
        out_shape=[jax.ShapeDtypeStruct((meta['p_max'], dm), BF16),
                   jax.ShapeDtypeStruct((TOP_K, n), I32)],
        compiler_params=_cparams("arbitrary"),
        name="moe_dispatch",
    )(*meta['copy_lists'], meta['nch'], meta['tail0'], meta['tailn'], *hn_parts, ti, meta['locv'])


def _expert_kernel(te_ref, nt_ref, run_ref, rexp_ref, nrun_ref, x_ref, b1_ref, bd_ref,
                   wu_hbm, wd_hbm, y_ref, wu_buf, wd_buf, w1_ref, w2_ref, act_ref, sem, *, layer):
    i = pl.program_id(0)
    live = i < nt_ref[0]
    run = run_ref[i]
    new_run = (i == 0) | (run_ref[jnp.maximum(i - 1, 0)] != run)
    pair = 2 * LANES
    nblk = wu_buf.shape[-1] // pair

    def weight_copies(expert, slot):
        return (pltpu.make_async_copy(wu_hbm.at[layer, expert], wu_buf.at[slot], sem.at[0, slot]),
                pltpu.make_async_copy(wd_hbm.at[layer, expert], wd_buf.at[slot], sem.at[1, slot]))

    @pl.when(live & new_run)
    def _():
        slot = run % 2

        @pl.when(run == 0)
        def _():
            for cp in weight_copies(rexp_ref[0], 0):
                cp.start()

        for cp in weight_copies(rexp_ref[run], slot):
            cp.wait()

        @pl.when(run + 1 < nrun_ref[0])
        def _():
            for cp in weight_copies(rexp_ref[run + 1], 1 - slot):
                cp.start()

        rr = lax.broadcasted_iota(I32, (pair, pair), 0)
        cc = lax.broadcasted_iota(I32, (pair, pair), 1)
        src = jnp.where(cc < LANES, 2 * cc, 2 * (cc - LANES) + 1)
        perm = jnp.where(rr == src, 1.0, 0.0).astype(BF16)
        for jb in range(nblk):
            cols = slice(jb * pair, (jb + 1) * pair)
            w1_ref[:, cols] = jnp.dot(wu_buf[slot, :, cols].astype(BF16), perm,
                                      preferred_element_type=F32).astype(BF16)
        w2_ref[...] = wd_buf[slot].astype(BF16)

    @pl.when(live)
    def _():
        x = x_ref[...]
        for jb in range(nblk):
            cols = slice(jb * pair, (jb + 1) * pair)
            hb = jnp.dot(x, w1_ref[:, cols], preferred_element_type=F32) + b1_ref[0, :, cols]
            gate = jnp.minimum(hb[:, :LANES], SWIGLU_LIMIT)
            up = jnp.clip(hb[:, LANES:], -SWIGLU_LIMIT, SWIGLU_LIMIT)
            act = gate * _sigmoid(SWIGLU_ALPHA * gate) * (up + 1.0)
            act_ref[:, jb * LANES:(jb + 1) * LANES] = act.astype(BF16)
        y = jnp.dot(act_ref[...], w2_ref[...], preferred_element_type=F32) + bd_ref[0, 0]
        y_ref[...] = y.astype(y_ref.dtype)


def _experts(xs, w_up, b1p, w_down, b_down, meta, layer):
    p_max, dm = xs.shape
    _, ne, _, f2 = w_up.shape
    f = f2 // 2
    n_tiles = p_max // EXPERT_TILE
    tile = lambda i, te, nt, *_: (jnp.minimum(i, jnp.maximum(nt[0] - 1, 0)), 0)
    byexp = lambda i, te, *_: (te[i], 0, 0)
    bylayer = lambda i, te, *_: (layer, te[i], 0, 0)
    gs = pltpu.PrefetchScalarGridSpec(
        num_scalar_prefetch=5,
        grid=(n_tiles,),
        in_specs=[pl.BlockSpec((EXPERT_TILE, dm), tile),
                  pl.BlockSpec((1, 1, f2), byexp), pl.BlockSpec((1, 1, 1, dm), bylayer),
                  pl.BlockSpec(memory_space=pl.ANY), pl.BlockSpec(memory_space=pl.ANY)],
        out_specs=pl.BlockSpec((EXPERT_TILE, dm), tile),
        scratch_shapes=[pltpu.VMEM((2, dm, f2), F32), pltpu.VMEM((2, f, dm), F32),
                        pltpu.VMEM((dm, f2), BF16), pltpu.VMEM((f, dm), BF16),
                        pltpu.VMEM((EXPERT_TILE, f), BF16),
                        pltpu.SemaphoreType.DMA((2, 2))],
    )
    return pl.pallas_call(
        functools.partial(_expert_kernel, layer=layer),
        grid_spec=gs,
        out_shape=jax.ShapeDtypeStruct((p_max, dm), BF16),
        compiler_params=_cparams("arbitrary"),
        name="moe_experts",
    )(meta['tile_e'], meta['n_tiles'], meta['tile_run'], meta['run_e'], meta['n_runs'],
      xs, b1p, b_down, w_up, w_down)


def _combine_kernel(*refs, s0, nb, final_norm):
    lists = _CopyLists(refs[:6])
    x_ref, slot_ref, tw_ref, g2_ref, gfin_ref, ys_ref, o_ref, yl_ref, sem = refs[6:]
    i = pl.program_id(0)
    ni = pl.num_programs(0)
    buf = i % 2

    def fetch(step, b):
        lists.start(s0 + step, yl_ref.at[b], ys_ref, sem.at[b], False)

    @pl.when(i == 0)
    def _():
        yl_ref[...] = jnp.zeros_like(yl_ref)
        fetch(0, 0)

    @pl.when(i + 1 < ni)
    def _():
        fetch(i + 1, 1 - buf)

    lists.wait(s0 + i, ys_ref, yl_ref.at[buf], sem.at[buf])

    slot = slot_ref[...].astype(F32)
    tw = tw_ref[...]
    sidx = lax.broadcasted_iota(I32, (SUB, SLOT_ROWS), 1).astype(F32)
    pw = jnp.where(sidx == slot[:, 0:1], tw[:, 0:1], 0.0)
    for k in range(1, TOP_K):
        pw = pw + jnp.where(sidx == slot[:, k:k + 1], tw[:, k:k + 1], 0.0)
    f = jnp.dot(pw.astype(BF16), yl_ref[buf], preferred_element_type=F32)
    dm = f.shape[-1]
    x2 = x_ref[...].reshape(SUB // nb, nb, dm) + g2_ref[...][None] * f.reshape(SUB // nb, nb, dm)
    x2 = x2.reshape(SUB, dm)
    if final_norm:
        x2 = x2 * lax.rsqrt(jnp.mean(x2 * x2, axis=-1, keepdims=True) + EPS) * gfin_ref[...]
    o_ref[...] = x2


def _combine(x2d, slot_t, tw_t, g2, gfin, ys, meta, *, s0, nb, final_norm):
    n, dm = x2d.shape
    ns = n // SUB
    gs = pltpu.PrefetchScalarGridSpec(
        num_scalar_prefetch=6,
        grid=(ns,),
        in_specs=[pl.BlockSpec((SUB, dm), lambda i, *_: (i, 0)),
                  pl.BlockSpec((SUB, TOP_K), lambda i, *_: (s0 + i, 0)),
                  pl.BlockSpec((SUB, TOP_K), lambda i, *_: (s0 + i, 0)),
                  pl.BlockSpec(g2.shape, lambda i, *_: (0, 0)),
                  pl.BlockSpec(gfin.shape, lambda i, *_: (0, 0)),
                  pl.BlockSpec(memory_space=pl.ANY)],
        out_specs=pl.BlockSpec((SUB, dm), lambda i, *_: (i, 0)),
        scratch_shapes=[pltpu.VMEM((2, SLOT_ROWS, dm), BF16),
                        pltpu.SemaphoreType.DMA((2, 2))],
    )
    return pl.pallas_call(
        functools.partial(_combine_kernel, s0=s0, nb=nb, final_norm=final_norm),
        grid_spec=gs,
        out_shape=jax.ShapeDtypeStruct((n, dm), F32),
        compiler_params=_cparams("arbitrary"),
        name="moe_combine",
    )(*meta['copy_lists'], x2d, slot_t, tw_t, g2, gfin, ys)


def _routing_meta(cnt, ne):
    ns = cnt.shape[0]
    pc = (cnt + (SEG_PAD - 1)) // SEG_PAD * SEG_PAD
    loc = jnp.cumsum(pc, axis=1) - pc
    tot = jnp.sum(pc, axis=0)
    reg = (tot + (EXPERT_TILE - 1)) // EXPERT_TILE * EXPERT_TILE
    reg_end = jnp.cumsum(reg)
    base = reg_end - reg
    glob = base[None, :] + jnp.cumsum(pc, axis=0) - pc
    p_max = ns * (SUB * TOP_K + ne * (SEG_PAD - 1)) + ne * (EXPERT_TILE - SEG_PAD)
    p_max = (p_max + EXPERT_TILE - 1) // EXPERT_TILE * EXPERT_TILE
    n_tiles_max = p_max // EXPERT_TILE
    tile_start = jnp.arange(n_tiles_max, dtype=I32) * EXPERT_TILE
    tile_e = jnp.sum((tile_start[:, None] >= reg_end[None, :]).astype(I32), axis=1)
    tile_e = jnp.minimum(tile_e, ne - 1)
    eids = jnp.arange(ne, dtype=I32)
    has = reg > 0
    run_of_e = jnp.cumsum(has.astype(I32)) - 1
    run_e = jnp.sum(jnp.where(has[None, :] & (run_of_e[None, :] == eids[:, None]), eids[None, :], 0),
                    axis=1)
    tile_run = jnp.sum(jnp.where(tile_e[:, None] == eids[None, :], run_of_e[None, :], 0), axis=1)
    crow = jnp.arange(MAX_CHUNKS, dtype=I32) * SEG_PAD
    loc_end = loc + pc
    ce = jnp.sum((crow[None, :, None] >= loc_end[:, None, :]).astype(I32), axis=2)
    ce = jnp.minimum(ce, ne - 1)
    pick = ce[:, :, None] == jnp.arange(ne, dtype=I32)[None, None, :]
    cdst = crow[None, :] + jnp.sum(jnp.where(pick, (glob - loc)[:, None, :], 0), axis=2)
    nch = jnp.sum(pc, axis=1) // SEG_PAD
    seg_first = jnp.sum(jnp.where(pick, loc[:, None, :], 0), axis=2)
    seg_rows = jnp.sum(jnp.where(pick, pc[:, None, :], 0), axis=2)
    row_in_seg = crow[None, :] - seg_first
    even = (crow[None, :] < (nch * SEG_PAD)[:, None]) & (row_in_seg % (2 * SEG_PAD) == 0)
    is_pair = even & (row_in_seg + 2 * SEG_PAD <= seg_rows)
    is_single = even & (row_in_seg + 2 * SEG_PAD > seg_rows)

    def compact(mask):
        rank = jnp.cumsum(mask.astype(I32), axis=1) - 1
        sel = mask[:, None, :] & (rank[:, None, :] == jnp.arange(MAX_PAIRS, dtype=I32)[None, :, None])
        lrow = jnp.sum(jnp.where(sel, crow[None, None, :], 0), axis=2)
        grow = jnp.sum(jnp.where(sel, cdst[:, None, :], 0), axis=2)
        return [lrow.reshape(-1).astype(I32), grow.reshape(-1).astype(I32),
                jnp.sum(mask.astype(I32), axis=1)]

    return dict(copy_lists=compact(is_pair) + compact(is_single),
                nch=(jnp.sum(pc, axis=1) // SEG_PAD).astype(I32),
                tail0=(base + tot).astype(I32), tailn=((reg - tot) // SEG_PAD).astype(I32),
                locv=loc.reshape(ns, ne, 1).astype(I32),
                tile_e=tile_e, n_tiles=(reg_end[-1:] // EXPERT_TILE).astype(I32), p_max=p_max,
                tile_run=jnp.maximum(tile_run, 0).astype(I32), run_e=run_e.astype(I32),
                n_runs=jnp.sum(has.astype(I32))[None])


def kernel(x, c, ctx, c_ctx, w_ada, b_ada, g_mix, g_ffn, w_in, s5_lam_re, s5_lam_im, s5_log_dt, s5_b_re, s5_b_im, s5_c_re, s5_c_im, s5_d, s5_w_glu, s5_b_glu, lru_conv_w, lru_conv_b, lru_lam, lru_w_a, lru_b_a, lru_w_x, lru_b_x, w_proj_s5, w_proj_lru, w_out, w_router, b_router, w_up, b_up, w_down, b_down, g_final):
    nb, seq, dm = x.shape
    tc = ctx.shape[1]
    depth = w_ada.shape[0]
    r = seq // GRID_W
    s5w = s5_d.shape[-1]
    lw = lru_conv_w.shape[-1]
    ne = w_router.shape[-1]
    nl, ncx = seq * nb, tc * nb

    xl = jnp.transpose(x, (1, 0, 2)).reshape(nl, dm)
    xc = jnp.transpose(ctx, (1, 0, 2)).reshape(ncx, dm)

    pad_rows = (-(nb + 1)) % 8
    cvec = jnp.concatenate([c, c_ctx[None, :], jnp.zeros((pad_rows, dm), F32)], axis=0)
    ada = _ada_all(cvec, w_ada, b_ada)

    s5_ops = jax.vmap(_s5_operators)(s5_lam_re, s5_lam_im, s5_log_dt, s5_b_re, s5_b_im,
                                     s5_c_re, s5_c_im)
    lam = lru_lam.astype(F32)
    softplus = jnp.maximum(-lam, 0.0) + jnp.log1p(jnp.exp(-jnp.abs(lam)))
    sp_all = (-0.5 * LRU_C) * softplus[:, :, None, :]
    wg_all = jax.vmap(_lru_gate_weights)(0.5 * lru_w_a, 0.5 * lru_w_x)
    b_a_all = 0.5 * lru_b_a[:, :, None, :]
    b_x_all = 0.5 * lru_b_x[:, :, None, :]

    for l in range(depth):
        need_ctx = l < depth - 1
        mods_l =[ada[l, :nb, k * dm:(k + 1) * dm] for k in range(6)]
        mods_c = [jnp.broadcast_to(ada[l, nb:nb + 1, k * dm:(k + 1) * dm], (nb, dm)) for k in range(6)]
        gm = g_mix[l][None, :]
        gf = g_ffn[l][None, :]
        w_in_l = w_in[l].astype(BF16)

        u_l, xr_l, gg_l, m5_l, ml_l = _k1(xl, gm, mods_l[0], mods_l[1], w_in_l, nb=nb, r=r,
                                          colmajor=True, s5w=s5w, lw=lw, need_gates=True)
        outs_c = _k1(xc, gm, mods_c[0], mods_c[1], w_in_l, nb=nb, r=r, colmajor=False,
                     s5w=s5w, lw=lw, need_gates=need_ctx)
        u_c, xr_c = outs_c[0], outs_c[1]

        bm, cm, a_re, a_im = [a[l] for a in s5_ops]
        nslab = s5w // LANES
        h0 = jnp.zeros((2, nslab, nb, 2 * S5_SLAB_STATE), F32)
        ys_c, hs_c = _s5_scan(u_c.reshape(tc, nb, s5w), bm, cm, a_re, a_im, h0)
        ys_l, _ = _s5_scan(u_l.reshape(seq, nb, s5w), bm, cm, a_re, a_im, hs_c)

        sp, wg, b_a, b_x = sp_all[l], wg_all[l], b_a_all[l], b_x_all[l]
        cw = lru_conv_w[l]
        cbias = lru_conv_b[l][None, :]
        hz = jnp.zeros((2, nb, lw), F32)
        cb_c = 2 if (tc // r) % 2 == 0 else 1
        cb_l = 2 if GRID_W % 2 == 0 else 1
        yl_c, hl_c = _lru_scan(xr_c, cw, cbias, sp, wg, b_a, b_x, hz, cb=cb_c)
        yl_l, _ = _lru_scan(xr_l, cw, cbias, sp, wg, b_a, b_x, hl_c, cb=cb_l)

        k2_params = (s5_d[l][None, :], s5_w_glu[l].astype(BF16), s5_b_glu[l][None, :],
                     w_proj_s5[l].astype(BF16), w_proj_lru[l].astype(BF16), w_out[l].astype(BF16))
        wr_t = jnp.transpose(w_router[l])
        wr_hi = wr_t.astype(BF16)
        wr_lo = (wr_t - wr_hi.astype(F32)).astype(BF16)
        br = b_router[l][:, None]
        x1_l, hn_l, ti_l, tw_l, cnt_l = _k2(xl, u_l, ys_l, yl_l, gg_l, m5_l, ml_l, *k2_params,
                                            mods_l[2], gf, mods_l[3], mods_l[4], wr_hi, wr_lo, br,
                                            nb=nb, r=r, colmajor=True)
        if need_ctx:
            _, _, gg_c, m5_c, ml_c = outs_c
            x1_c, hn_c, ti_c, tw_c, cnt_c = _k2(xc, u_c, ys_c, yl_c, gg_c, m5_c, ml_c, *k2_params,
                                                mods_c[2], gf, mods_c[3], mods_c[4], wr_hi, wr_lo, br,
                                                nb=nb, r=r, colmajor=False)
            hn_parts = [hn_c, hn_l]
            ti = jnp.concatenate([ti_c, ti_l], axis=1)
            tw = jnp.concatenate([tw_c, tw_l], axis=1)
            cnt = jnp.concatenate([cnt_c, cnt_l], axis=0)
        else:
            hn_parts, ti, tw, cnt = [hn_l], ti_l, tw_l, cnt_l

        meta = _routing_meta(cnt[:, :, 0], ne)
        xs, slots = _dispatch(hn_parts, ti, meta, ne=ne)
        f2 = w_up.shape[-1]
        b1p = jnp.transpose(b_up[l].reshape(ne, f2 // (2 * LANES), LANES, 2), (0, 1, 3, 2))
        ys = _experts(xs, w_up, b1p.reshape(ne, 1, f2), w_down, b_down[:, :, None, :], meta, l)
        slot_t = jnp.transpose(slots)
        tw_t = jnp.transpose(tw)
        last = l == depth - 1
        gfin = g_final[None, :]
        if need_ctx:
            xc = _combine(x1_c, slot_t, tw_t, mods_c[5], gfin, ys, meta, s0=0, nb=nb,
                          final_norm=False)
            xl = _combine(x1_l, slot_t, tw_t, mods_l[5], gfin, ys, meta, s0=ncx // SUB,
                          nb=nb, final_norm=False)
        else:
            xl = _combine(x1_l, slot_t, tw_t, mods_l[5], gfin, ys, meta, s0=0, nb=nb,
                          final_norm=last)

    return jnp.transpose(xl.reshape(seq, nb, dm), (1, 0, 2)).astype(x.dtype)
```

```python
import functools
import math

import jax
import jax.numpy as jnp
from jax import lax
from jax.experimental import pallas as pl
from jax.experimental.pallas import tpu as pltpu

F32 = jnp.float32
BF16 = jnp.bfloat16
I32 = jnp.int32
HIGHEST = lax.Precision.HIGHEST

GRID_W = 64
S5_GROUP = 16
S5_STATE = 64
S5_DT_MIN = 0.001
S5_DT_MAX = 0.1
LRU_BLOCK = 64
LRU_C = 8.0
CONV_W = 4
TOP_K = 4
SWIGLU_LIMIT = 7.0
SWIGLU_ALPHA = 1.702
EPS = 1e-6

LANES = 128
BF16_ROWS = 16
VMEM_LIMIT = 56 * 1024 * 1024

ROW_TILE = 1024
K2_TILE = 512
S5_SLAB_GROUPS = LANES // S5_GROUP
S5_SLAB_STATE = S5_SLAB_GROUPS * S5_STATE
SUB = 256
SEG_PAD = BF16_ROWS
SLOT_ROWS = 1536
MAX_CHUNKS = SLOT_ROWS // SEG_PAD
MAX_PAIRS = MAX_CHUNKS // 2
SHORT_ROWS = SUB * TOP_K + 256
EXPERT_TILE = 1024


def _cparams(*sem, **kw):
    return pltpu.CompilerParams(dimension_semantics=sem, vmem_limit_bytes=VMEM_LIMIT, **kw)


def _gelu(x):
    c0 = math.sqrt(2.0 / math.pi)
    half = 0.5 * x
    return half + half * jnp.tanh(x * (c0 + (c0 * 0.044715) * (x * x)))


def _sigmoid(x):
    return 0.5 * jnp.tanh(0.5 * x) + 0.5


def _rms_mod(x, g, sh, sc, nb):
    tm, d = x.shape
    y = x * lax.rsqrt(jnp.mean(x * x, axis=-1, keepdims=True) + EPS) * g
    y3 = y.reshape(tm // nb, nb, d) * (1.0 + sc)[None] + sh[None]
    return y3.reshape(tm, d)


def _ada_kernel(c_ref, w_ref, b_ref, o_ref):
    c = c_ref[...]
    cond = c * _sigmoid(c)
    o_ref[0] = jnp.dot(cond, w_ref[0], preferred_element_type=F32, precision=HIGHEST) + b_ref[0]


def _ada_all(cvec, w_ada, b_ada):
    nl, d, w6 = w_ada.shape
    r = cvec.shape[0]
    tn = min(w6, 1536)
    return pl.pallas_call(
        _ada_kernel,
        grid=(nl, w6 // tn),
        in_specs=[pl.BlockSpec((r, d), lambda l, j: (0, 0)),
                  pl.BlockSpec((1, d, tn), lambda l, j: (l, 0, j)),
                  pl.BlockSpec((1, 1, tn), lambda l, j: (l, 0, j))],
        out_specs=pl.BlockSpec((1, r, tn), lambda l, j: (l, 0, j)),
        out_shape=jax.ShapeDtypeStruct((nl, r, w6), F32),
        compiler_params=_cparams("arbitrary", "arbitrary"),
        name="ada",
    )(cvec, w_ada, b_ada.reshape(nl, 1, w6))


def _k1(x2, g, sh, sc, w_in, *, nb, r, colmajor, s5w, lw, need_gates):
    n, d = x2.shape
    t = n // nb
    ncol = t // r
    if colmajor:
        tm = (t // r) * nb
        xr_spec = pl.BlockSpec((ncol, 1, nb, lw), lambda i: (0, i, 0, 0))
    else:
        cbk = max(1, min(ncol, ROW_TILE // (r * nb)))
        tm = cbk * r * nb
        xr_spec = pl.BlockSpec((cbk, r, nb, lw), lambda i: (i, 0, 0, 0))
    splits = (s5w, s5w + lw, s5w + 2 * lw, s5w + 2 * lw + d, s5w + 2 * lw + 2 * d)
    if not need_gates:
        splits = splits[:2]
    row = lambda w: pl.BlockSpec((tm, w), lambda i: (i, 0))
    full = lambda a: pl.BlockSpec(a.shape, lambda i: (0,) * a.ndim)
    out_specs = [row(s5w), xr_spec, row(lw), row(d), row(d)]
    out_shape = [jax.ShapeDtypeStruct((n, s5w), F32),
                 jax.ShapeDtypeStruct((ncol, r, nb, lw), F32),
                 jax.ShapeDtypeStruct((n, lw), BF16),
                 jax.ShapeDtypeStruct((n, d), BF16),
                 jax.ShapeDtypeStruct((n, d), BF16)]
    k = len(splits)

    def body(x_ref, g_ref, sh_ref, sc_ref, w_ref, *o_refs):
        hn = _rms_mod(x_ref[...], g_ref[...], sh_ref[...], sc_ref[...], nb).astype(BF16)
        lo = 0
        for o_ref, hi in zip(o_refs, splits):
            v = jnp.dot(hn, w_ref[:, lo:hi], preferred_element_type=F32).astype(o_ref.dtype)
            o_ref[...] = v.reshape(o_ref.shape)
            lo = hi

    w_used = w_in[:, :splits[-1]]
    return pl.pallas_call(
        body,
        grid=(n // tm,),
        in_specs=[row(d), full(g), full(sh), full(sc), full(w_used)],
        out_specs=out_specs[:k],
        out_shape=out_shape[:k],
        compiler_params=_cparams("arbitrary"),
        name="k1_in_proj",
    )(x2, g, sh, sc, w_used)


def _s5_kernel(u_ref, bm_ref, cm_ref, are_ref, aim_ref, h0_ref, y_ref, ht_ref, hs_ref, st_ref,
               *, tt, nb, nslab):
    d = pl.program_id(0)
    j = pl.program_id(1)
    nj = pl.num_programs(1)
    sw = S5_SLAB_STATE
    rows = tt * nb

    @pl.when(j == 0)
    def _():
        st_ref[...] = h0_ref[0]

    def project_in(m):
        ub = u_ref[:, :, m * LANES:(m + 1) * LANES].reshape(rows, LANES).astype(BF16)
        hs_ref[m] = jnp.dot(ub, bm_ref[0, m], preferred_element_type=F32)

    def recur(m, reverse):
        a_re = jnp.broadcast_to(are_ref[0, m], (nb, sw))
        a_im = jnp.broadcast_to(aim_ref[0, m], (nb, sw))
        hr = st_ref[m, :, :sw]
        hi = st_ref[m, :, sw:]
        for i in range(tt):
            t = tt - 1 - i if reverse else i
            rs = slice(t * nb, (t + 1) * nb)
            nr = a_re * hr - a_im * hi + hs_ref[m, rs, :sw]
            ni = a_re * hi + a_im * hr + hs_ref[m, rs, sw:]
            hs_ref[m, rs, :sw] = nr
            hs_ref[m, rs, sw:] = ni
            hr, hi = nr, ni
        st_ref[m, :, :sw] = hr
        st_ref[m, :, sw:] = hi

    def project_out(m):
        y = jnp.dot(hs_ref[m].astype(BF16), cm_ref[0, m], preferred_element_type=F32)
        y_ref[0, :, :, m * LANES:(m + 1) * LANES] = y.reshape(tt, nb, LANES)

    def run(reverse):
        project_in(0)
        for m in range(nslab):
            if m + 1 < nslab:
                project_in(m + 1)
            recur(m, reverse)
            if m > 0:
                project_out(m - 1)
        project_out(nslab - 1)

    @pl.when(d == 0)
    def _():
        run(False)

    @pl.when(d == 1)
    def _():
        run(True)

    @pl.when(j == nj - 1)
    def _():
        ht_ref[0] = st_ref[...]


def _s5_scan(u3, bm, cm, a_re, a_im, h0):
    t, nb, s5w = u3.shape
    nslab = s5w // LANES
    tt = 64 if t % 64 == 0 else t
    nj = t // tt
    sw2 = 2 * S5_SLAB_STATE
    blk = lambda d, j: j + d * (nj - 1 - 2 * j)
    return pl.pallas_call(
        functools.partial(_s5_kernel, tt=tt, nb=nb, nslab=nslab),
        grid=(2, nj),
        in_specs=[pl.BlockSpec((tt, nb, s5w), lambda d, j: (blk(d, j), 0, 0)),
                  pl.BlockSpec((1, nslab, LANES, sw2), lambda d, j: (d, 0, 0, 0)),
                  pl.BlockSpec((1, nslab, sw2, LANES), lambda d, j: (d, 0, 0, 0)),
                  pl.BlockSpec((1, nslab, 1, S5_SLAB_STATE), lambda d, j: (d, 0, 0, 0)),
                  pl.BlockSpec((1, nslab, 1, S5_SLAB_STATE), lambda d, j: (d, 0, 0, 0)),
                  pl.BlockSpec((1, nslab, nb, sw2), lambda d, j: (d, 0, 0, 0))],
        out_specs=[pl.BlockSpec((1, tt, nb, s5w), lambda d, j: (d, blk(d, j), 0, 0)),
                   pl.BlockSpec((1, nslab, nb, sw2), lambda d, j: (d, 0, 0, 0))],
        out_shape=[jax.ShapeDtypeStruct((2, t, nb, s5w), F32),
                   jax.ShapeDtypeStruct((2, nslab, nb, sw2), F32)],
        scratch_shapes=[pltpu.VMEM((nslab, tt * nb, sw2), F32),
                        pltpu.VMEM((nslab, nb, sw2), F32)],
        compiler_params=_cparams("arbitrary", "arbitrary"),
        name="s5_scan",
    )(u3, bm, cm, a_re, a_im, h0)


def _s5_operators(lam_re, lam_im, log_dt, b_re, b_im, c_re, c_im):
    lr, li = lam_re.astype(F32), lam_im.astype(F32)
    dt = jnp.exp(log_dt.astype(F32))[..., None]
    mag = jnp.exp(lr * dt)
    ab_re, ab_im = mag * jnp.cos(li * dt), mag * jnp.sin(li * dt)
    den = lr * lr + li * li
    zr = ab_re - 1.0
    q_re = (zr * lr + ab_im * li) / den
    q_im = (ab_im * lr - zr * li) / den
    br, bi = b_re.astype(F32), b_im.astype(F32)
    bb_re = q_re[..., None] * br - q_im[..., None] * bi
    bb_im = q_re[..., None] * bi + q_im[..., None] * br
    two, g, p, h = bb_re.shape
    gs = S5_SLAB_GROUPS
    nslab = g // gs
    eye = jnp.eye(gs, dtype=F32)

    def in_mat(bb):
        bb = bb.reshape(two, nslab, gs, p, h)
        m = jnp.einsum('dmgph,gk->dmghkp', bb, eye)
        return m.reshape(two, nslab, gs * h, gs * p)

    def out_mat(cc):
        cc = cc.astype(F32).reshape(two, nslab, gs, h, p)
        m = jnp.einsum('dmghp,gk->dmgpkh', cc, eye)
        return m.reshape(two, nslab, gs * p, gs * h)

    bm = jnp.concatenate([in_mat(bb_re), in_mat(bb_im)], axis=-1).astype(BF16)
    cm = jnp.concatenate([out_mat(c_re), -out_mat(c_im)], axis=-2).astype(BF16)
    a_re = ab_re.reshape(two, nslab, 1, gs * p)
    a_im = ab_im.reshape(two, nslab, 1, gs * p)
    return bm, cm, a_re, a_im


def _lru_kernel(x_ref, xp_ref, xn_ref, cw_ref, cb_ref, c1_ref, wg_ref, ba_ref, bx_ref, h0_ref,
                y_ref, ht_ref, xpad_ref, xc_ref, a_ref, b_ref, st_ref, *, cb, r, nb):
    d = pl.program_id(0)
    j = pl.program_id(1)
    nj = pl.num_programs(1)
    jeff = j + d * (nj - 1 - 2 * j)
    steps = cb * r
    rows = steps * nb
    w = x_ref.shape[-1]

    @pl.when(j == 0)
    def _():
        st_ref[...] = h0_ref[0]

    has_prev = (jeff > 0).astype(F32)
    has_next = (jeff < nj - 1).astype(F32)
    xpad_ref[0:nb, :] = xp_ref[0, 0].astype(F32) * has_prev
    xpad_ref[nb:nb + rows, :] = x_ref[...].reshape(rows, w).astype(F32)
    xpad_ref[nb + rows:3 * nb + rows, :] = xn_ref[0].reshape(2 * nb, w).astype(F32) * has_next
    acc = cb_ref[...] + cw_ref[0:1, :] * xpad_ref[0:rows, :]
    for k in range(1, CONV_W):
        acc = acc + cw_ref[k:k + 1, :] * xpad_ref[k * nb:k * nb + rows, :]
    xc_ref[...] = acc

    for s in range(w // LANES):
        ls = slice(s * LANES, (s + 1) * LANES)
        xs = xc_ref[:, ls]
        gs = jnp.dot(xs.astype(BF16), wg_ref[0, s], preferred_element_type=F32)
        tr = jnp.tanh(gs[:, :LANES] + ba_ref[0, :, ls])
        ig = 0.5 * jnp.tanh(gs[:, LANES:] + bx_ref[0, :, ls]) + 0.5
        c1 = c1_ref[0, :, ls]
        a = jnp.exp(c1 * tr + c1)
        a_ref[:, ls] = a
        s1 = 1.0 - a * a
        root = jnp.where(s1 > 0.0, s1 * lax.rsqrt(s1), 0.0)
        b_ref[:, ls] = root * (ig * xs)

    def step(i, h):
        t = jnp.where(d == 0, i, steps - 1 - i)
        r0 = pl.multiple_of(t * nb, nb)
        h = a_ref[pl.ds(r0, nb), :] * h + b_ref[pl.ds(r0, nb), :]
        y_ref[0, t // r, t % r] = h
        return h

    st_ref[...] = lax.fori_loop(0, steps, step, st_ref[...], unroll=8)

    @pl.when(j == nj - 1)
    def _():
        ht_ref[0] = st_ref[...]


def _lru_scan(x4, conv_w, conv_b, sp, wg, b_a, b_x, h0, *, cb):
    c, r, nb, w = x4.shape
    nj = c // cb
    rows = cb * r * nb
    blk = lambda d, j: j + d * (nj - 1 - 2 * j)
    full2 = lambda a: pl.BlockSpec(a.shape, lambda d, j: (0,) * a.ndim)
    perdir = lambda a: pl.BlockSpec((1,) + a.shape[1:], lambda d, j: (d,) + (0,) * (a.ndim - 1))
    return pl.pallas_call(
        functools.partial(_lru_kernel, cb=cb, r=r, nb=nb),
        grid=(2, nj),
        in_specs=[pl.BlockSpec((cb, r, nb, w), lambda d, j: (blk(d, j), 0, 0, 0)),
                  pl.BlockSpec((1, 1, nb, w),
                               lambda d, j: (jnp.maximum(blk(d, j) * cb - 1, 0), r - 1, 0, 0)),
                  pl.BlockSpec((1, 2, nb, w),
                               lambda d, j: (jnp.minimum((blk(d, j) + 1) * cb, c - 1), 0, 0, 0)),
                  full2(conv_w), full2(conv_b), perdir(sp), perdir(wg), perdir(b_a), perdir(b_x),
                  perdir(h0)],
        out_specs=[pl.BlockSpec((1, cb, r, nb, w), lambda d, j: (d, blk(d, j), 0, 0, 0)),
                   pl.BlockSpec((1, nb, w), lambda d, j: (d, 0, 0))],
        out_shape=[jax.ShapeDtypeStruct((2, c, r, nb, w), F32),
                   jax.ShapeDtypeStruct((2, nb, w), F32)],
        scratch_shapes=[pltpu.VMEM((rows + 3 * nb, w), F32),
                        pltpu.VMEM((rows, w), F32),
                        pltpu.VMEM((rows, w), F32),
                        pltpu.VMEM((rows, w), F32),
                        pltpu.VMEM((nb, w), F32)],
        compiler_params=_cparams("arbitrary", "arbitrary"),
        name="lru_scan",
    )(x4, x4, x4, conv_w, conv_b, sp, wg, b_a, b_x, h0)


def _lru_gate_weights(w_a, w_x):
    two, nblk, k, _ = w_a.shape
    per = LANES // k
    eye = jnp.eye(per, dtype=F32)

    def slabs(wm):
        wm = wm.astype(F32).reshape(two, nblk // per, per, k, k)
        m = jnp.einsum('dsgij,gk->dsgikj', wm, eye)
        return m.reshape(two, nblk // per, LANES, LANES)

    return jnp.concatenate([slabs(w_a), slabs(w_x)], axis=-1).astype(BF16)


def _k2_kernel(x_ref, u_ref, ys_ref, yl_ref, gg_ref, m5_ref, ml_ref, d5_ref, wglu_ref, bglu_ref,
               wp5_ref, wpl_ref, wo_ref, g1_ref, gf_ref, sh_ref, sc_ref, wrh_ref, wrl_ref, br_ref,
               x1_ref, hn_ref, ti_ref, tw_ref, cnt_ref, *, nb):
    tm, dm = x_ref.shape
    u = u_ref[...]
    y5 = ys_ref[0] + ys_ref[1] + d5_ref[...] * u
    gy = _gelu(y5)
    o5 = gy * _sigmoid(jnp.dot(gy.astype(BF16), wglu_ref[...], preferred_element_type=F32)
                       + bglu_ref[...])
    yl = (yl_ref[0] + yl_ref[1]).reshape(tm, -1)
    ol = _gelu(gg_ref[...].astype(F32)) * yl
    merged = (_sigmoid(m5_ref[...]).astype(F32)
              * jnp.dot(o5.astype(BF16), wp5_ref[...], preferred_element_type=F32)
              + _sigmoid(ml_ref[...]).astype(F32)
              * jnp.dot(ol.astype(BF16), wpl_ref[...], preferred_element_type=F32))
    mix = jnp.dot(merged.astype(BF16), wo_ref[...], preferred_element_type=F32)
    x1 = (x_ref[...].reshape(tm // nb, nb, dm) + g1_ref[...][None] * mix.reshape(tm // nb, nb, dm))
    x1 = x1.reshape(tm, dm)
    x1_ref[...] = x1
    hn = _rms_mod(x1, gf_ref[...], sh_ref[...], sc_ref[...], nb)
    hn_hi = hn.astype(BF16)
    hn_ref[...] = hn_hi
    hn_lo = (hn - hn_hi.astype(F32)).astype(BF16)

    nt = (((1,), (1,)), ((), ()))
    logits = (lax.dot_general(wrh_ref[...], hn_hi, nt, preferred_element_type=F32)
              + lax.dot_general(wrl_ref[...], hn_hi, nt, preferred_element_type=F32)
              + lax.dot_general(wrh_ref[...], hn_lo, nt, preferred_element_type=F32)) + br_ref[...]
    ne = logits.shape[0]
    eidx = lax.broadcasted_iota(I32, (ne, tm), 0).astype(F32)
    vals = logits
    tv, ti = [], []
    for _ in range(TOP_K):
        m = jnp.max(vals, axis=0, keepdims=True)
        idx = jnp.min(jnp.where(vals == m, eidx, float(ne)), axis=0, keepdims=True)
        tv.append(m)
        ti.append(idx)
        vals = jnp.where(eidx == idx, -jnp.inf, vals)
    ex = [jnp.exp(v - tv[0]) for v in tv]
    den = ex[0] + ex[1] + ex[2] + ex[3]
    ti_ref[...] = jnp.concatenate(ti, axis=0).astype(I32)
    tw_ref[...] = jnp.concatenate([e / den for e in ex], axis=0)
    sel = (eidx == ti[0]) | (eidx == ti[1]) | (eidx == ti[2]) | (eidx == ti[3])
    self32 = jnp.where(sel, 1.0, 0.0)
    for sb in range(tm // SUB):
        cnt_ref[sb] = jnp.sum(self32[:, sb * SUB:(sb + 1) * SUB], axis=1, keepdims=True).astype(I32)


def _k2(x2, u, ys, yl5, gg, m5, ml, d5, wglu, bglu, wp5, wpl, wo, g1, gf, sh, sc, wr_hi, wr_lo, br,
        *, nb, r, colmajor):
    n, dm = x2.shape
    t = n // nb
    ncol = t // r
    lw = yl5.shape[-1]
    s5w = u.shape[-1]
    ne = wr_hi.shape[0]
    if colmajor:
        cbk = max(1, min(ncol, K2_TILE // nb))
        per_row = ncol // cbk
        tm = cbk * nb
        yl_spec = pl.BlockSpec((2, cbk, 1, nb, lw),
                               lambda i: (0, i % per_row, i // per_row, 0, 0))
    else:
        cbk = max(1, min(ncol, K2_TILE // (r * nb)))
        tm = cbk * r * nb
        yl_spec = pl.BlockSpec((2, cbk, r, nb, lw), lambda i: (0, i, 0, 0, 0))
    row = lambda w: pl.BlockSpec((tm, w), lambda i: (i, 0))
    full = lambda a: pl.BlockSpec(a.shape, lambda i: (0,) * a.ndim)
    ys2 = ys.reshape(2, n, s5w)
    params = (d5, wglu, bglu, wp5, wpl, wo, g1, gf, sh, sc, wr_hi, wr_lo, br)
    return pl.pallas_call(
        functools.partial(_k2_kernel, nb=nb),
        grid=(n // tm,),
        in_specs=[row(dm), row(s5w), pl.BlockSpec((2, tm, s5w), lambda i: (0, i, 0)), yl_spec,
                  row(lw), row(dm), row(dm)] + [full(p) for p in params],
        out_specs=[row(dm), row(dm),
                   pl.BlockSpec((TOP_K, tm), lambda i: (0, i)),
                   pl.BlockSpec((TOP_K, tm), lambda i: (0, i)),
                   pl.BlockSpec((tm // SUB, ne, 1), lambda i: (i, 0, 0))],
        out_shape=[jax.ShapeDtypeStruct((n, dm), F32),
                   jax.ShapeDtypeStruct((n, dm), BF16),
                   jax.ShapeDtypeStruct((TOP_K, n), I32),
                   jax.ShapeDtypeStruct((TOP_K, n), F32),
                   jax.ShapeDtypeStruct((n // SUB, ne, 1), I32)],
        compiler_params=_cparams("arbitrary"),
        name="k2_mix_out",
    )(x2, u, ys2, yl5, gg, m5, ml, *params)


def _start_copies(n, rows, lrow_ref, grow_ref, base, local_ref, sorted_ref, sem, to_sorted):
    def one(k, _):
        lo = pl.multiple_of(lrow_ref[base + k], SEG_PAD)
        g = pl.multiple_of(grow_ref[base + k], SEG_PAD)
        loc = local_ref.at[pl.ds(lo, rows)]
        srt = sorted_ref.at[pl.ds(g, rows)]
        if to_sorted:
            pltpu.make_async_copy(loc, srt, sem).start()
        else:
            pltpu.make_async_copy(srt, loc, sem).start()
        return 0

    lax.fori_loop(0, n, one, 0)


def _wait_copies(n, rows, src_ref, dst_ref, sem):
    def one(k, _):
        pltpu.make_async_copy(src_ref.at[pl.ds(0, rows)], dst_ref.at[pl.ds(0, rows)], sem).wait()
        return 0

    lax.fori_loop(0, n, one, 0)


class _CopyLists:
    def __init__(self, refs):
        self.pl_ref, self.pg_ref, self.pn_ref, self.sl_ref, self.sg_ref, self.sn_ref = refs

    def start(self, s, local_ref, sorted_ref, sems, to_sorted):
        _start_copies(self.pn_ref[s], 2 * SEG_PAD, self.pl_ref, self.pg_ref, s * MAX_PAIRS,
                      local_ref, sorted_ref, sems.at[0], to_sorted)
        _start_copies(self.sn_ref[s], SEG_PAD, self.sl_ref, self.sg_ref, s * MAX_PAIRS,
                      local_ref, sorted_ref, sems.at[1], to_sorted)

    def wait(self, s, src_ref, dst_ref, sems):
        _wait_copies(self.pn_ref[s], 2 * SEG_PAD, src_ref, dst_ref, sems.at[0])
        _wait_copies(self.sn_ref[s], SEG_PAD, src_ref, dst_ref, sems.at[1])


def _dispatch_kernel(*refs, ne, part_starts):
    lists = _CopyLists(refs[:6])
    nch_ref, tail0_ref, tailn_ref = refs[6:9]
    nparts = len(part_starts)
    hn_refs = refs[9:9 + nparts]
    ti_ref, locv_ref, xs_ref, slot_ref, xg_ref, zero_ref, sem, tsem = refs[9 + nparts:]
    s = pl.program_id(0)
    ns = pl.num_programs(0)
    buf = s % 2

    @pl.when(s == 0)
    def _():
        zero_ref[...] = jnp.zeros_like(zero_ref)

        def per_e(e, _):
            def per_chunk(c, _):
                d0 = pl.multiple_of(tail0_ref[e] + c * SEG_PAD, SEG_PAD)
                pltpu.make_async_copy(zero_ref, xs_ref.at[pl.ds(d0, SEG_PAD)], tsem).start()
                return 0
            lax.fori_loop(0, tailn_ref[e], per_chunk, 0)
            return 0
        lax.fori_loop(0, ne, per_e, 0)

        def per_e_wait(e, _):
            def per_chunk(c, _):
                pltpu.make_async_copy(zero_ref, xs_ref.at[pl.ds(0, SEG_PAD)], tsem).wait()
                return 0
            lax.fori_loop(0, tailn_ref[e], per_chunk, 0)
            return 0
        lax.fori_loop(0, ne, per_e_wait, 0)

    ti = ti_ref[...].astype(F32)
    eidx = lax.broadcasted_iota(I32, (ne, SUB), 0).astype(F32)
    hit = [eidx == ti[k:k + 1] for k in range(TOP_K)]
    sel = hit[0] | hit[1] | hit[2] | hit[3]
    upper = (lax.broadcasted_iota(I32, (SUB, SUB), 0)
             < lax.broadcasted_iota(I32, (SUB, SUB), 1))
    rank = jnp.dot(jnp.where(sel, 1.0, 0.0).astype(BF16), jnp.where(upper, 1.0, 0.0).astype(BF16),
                   preferred_element_type=F32)
    slot_e = locv_ref[0].astype(F32) + rank
    slots = [jnp.sum(jnp.where(hit[k], slot_e, 0.0), axis=0, keepdims=True) for k in range(TOP_K)]
    slot_ref[...] = jnp.concatenate(slots, axis=0).astype(I32)

    hn = hn_refs[0][...]
    for p in range(1, nparts):
        hn = jnp.where(s >= part_starts[p], hn_refs[p][...], hn)
    blk = 256
    used_rows = nch_ref[s] * SEG_PAD

    def gather_rows(rb):
        sidx = (lax.broadcasted_iota(I32, (blk, SUB), 0) + rb * blk).astype(F32)
        p = (sidx == slots[0]) | (sidx == slots[1]) | (sidx == slots[2]) | (sidx == slots[3])
        xg = jnp.dot(jnp.where(p, 1.0, 0.0).astype(BF16), hn, preferred_element_type=F32)
        xg_ref[buf, rb * blk:(rb + 1) * blk, :] = xg.astype(BF16)

    for rb in range(SLOT_ROWS // blk):
        if (rb + 1) * blk <= SHORT_ROWS:
            gather_rows(rb)
        else:
            pl.when(used_rows > rb * blk)(functools.partial(gather_rows, rb))

    lists.start(s, xg_ref.at[buf], xs_ref, sem.at[buf], True)

    @pl.when(s > 0)
    def _():
        lists.wait(s - 1, xg_ref.at[1 - buf], xs_ref, sem.at[1 - buf])

    @pl.when(s == ns - 1)
    def _():
        lists.wait(s, xg_ref.at[buf], xs_ref, sem.at[buf])


def _dispatch(hn_parts, ti, meta, *, ne):
    dm = hn_parts[0].shape[1]
    counts = [h.shape[0] // SUB for h in hn_parts]
    starts = [sum(counts[:p]) for p in range(len(counts))]
    ns = sum(counts)
    n = ns * SUB

    def part_spec(start, count):
        return pl.BlockSpec((SUB, dm), lambda s, *_: (jnp.clip(s - start, 0, count - 1), 0))

    assert ne <= MAX_PAIRS
    gs = pltpu.PrefetchScalarGridSpec(
        num_scalar_prefetch=9,
        grid=(ns,),
        in_specs=[part_spec(st, ct) for st, ct in zip(starts, counts)]
                 + [pl.BlockSpec((TOP_K, SUB), lambda s, *_: (0, s)),
                    pl.BlockSpec((1, ne, 1), lambda s, *_: (s, 0, 0))],
        out_specs=[pl.BlockSpec(memory_space=pl.ANY),
                   pl.BlockSpec((TOP_K, SUB), lambda s, *_: (0, s))],
        scratch_shapes=[pltpu.VMEM((2, SLOT_ROWS, dm), BF16),
                        pltpu.VMEM((SEG_PAD, dm), BF16),
                        pltpu.SemaphoreType.DMA((2, 2)),
                        pltpu.SemaphoreType.DMA(())],
    )
    return pl.pallas_call(
        functools.partial(_dispatch_kernel, ne=ne, part_starts=tuple(starts)),
        grid_spec=gs,
        out_shape=[jax.ShapeDtypeStruct((meta['p_max'], dm), BF16),
                   jax.ShapeDtypeStruct((TOP_K, n), I32)],
        compiler_params=_cparams("arbitrary"),
        name="moe_dispatch",
    )(*meta['copy_lists'], meta['nch'], meta['tail0'], meta['tailn'], *hn_parts, ti, meta['locv'])


def _expert_kernel(te_ref, nt_ref, run_ref, rexp_ref, nrun_ref, x_ref, b1_ref, bd_ref,
                   wu_hbm, wd_hbm, y_ref, wu_buf, wd_buf, w1_ref, w2_ref, act_ref, sem, *, layer):
    i = pl.program_id(0)
    live = i < nt_ref[0]
    run = run_ref[i]
    new_run = (i == 0) | (run_ref[jnp.maximum(i - 1, 0)] != run)
    pair = 2 * LANES
    nblk = wu_buf.shape[-1] // pair

    def weight_copies(expert, slot):
        return (pltpu.make_async_copy(wu_hbm.at[layer, expert], wu_buf.at[slot], sem.at[0, slot]),
                pltpu.make_async_copy(wd_hbm.at[layer, expert], wd_buf.at[slot], sem.at[1, slot]))

    @pl.when(live & new_run)
    def _():
        slot = run % 2

        @pl.when(run == 0)
        def _():
            for cp in weight_copies(rexp_ref[0], 0):
                cp.start()

        for cp in weight_copies(rexp_ref[run], slot):
            cp.wait()

        @pl.when(run + 1 < nrun_ref[0])
        def _():
            for cp in weight_copies(rexp_ref[run + 1], 1 - slot):
                cp.start()

        rr = lax.broadcasted_iota(I32, (pair, pair), 0)
        cc = lax.broadcasted_iota(I32, (pair, pair), 1)
        src = jnp.where(cc < LANES, 2 * cc, 2 * (cc - LANES) + 1)
        perm = jnp.where(rr == src, 1.0, 0.0).astype(BF16)
        for jb in range(nblk):
            cols = slice(jb * pair, (jb + 1) * pair)
            w1_ref[:, cols] = jnp.dot(wu_buf[slot, :, cols].astype(BF16), perm,
                                      preferred_element_type=F32).astype(BF16)
        w2_ref[...] = wd_buf[slot].astype(BF16)

    @pl.when(live)
    def _():
        x = x_ref[...]
        for jb in range(nblk):
            cols = slice(jb * pair, (jb + 1) * pair)
            hb = jnp.dot(x, w1_ref[:, cols], preferred_element_type=F32) + b1_ref[0, :, cols]
            gate = jnp.minimum(hb[:, :LANES], SWIGLU_LIMIT)
            up = jnp.clip(hb[:, LANES:], -SWIGLU_LIMIT, SWIGLU_LIMIT)
            act = gate * _sigmoid(SWIGLU_ALPHA * gate) * (up + 1.0)
            act_ref[:, jb * LANES:(jb + 1) * LANES] = act.astype(BF16)
        y = jnp.dot(act_ref[...], w2_ref[...], preferred_element_type=F32) + bd_ref[0, 0]
        y_ref[...] = y.astype(y_ref.dtype)


def _experts(xs, w_up, b1p, w_down, b_down, meta, layer):
    p_max, dm = xs.shape
    _, ne, _, f2 = w_up.shape
    f = f2 // 2
    n_tiles = p_max // EXPERT_TILE
    tile = lambda i, te, nt, *_: (jnp.minimum(i, jnp.maximum(nt[0] - 1, 0)), 0)
    byexp = lambda i, te, *_: (te[i], 0, 0)
    bylayer = lambda i, te, *_: (layer, te[i], 0, 0)
    gs = pltpu.PrefetchScalarGridSpec(
        num_scalar_prefetch=5,
        grid=(n_tiles,),
        in_specs=[pl.BlockSpec((EXPERT_TILE, dm), tile),
                  pl.BlockSpec((1, 1, f2), byexp), pl.BlockSpec((1, 1, 1, dm), bylayer),
                  pl.BlockSpec(memory_space=pl.ANY), pl.BlockSpec(memory_space=pl.ANY)],
        out_specs=pl.BlockSpec((EXPERT_TILE, dm), tile),
        scratch_shapes=[pltpu.VMEM((2, dm, f2), F32), pltpu.VMEM((2, f, dm), F32),
                        pltpu.VMEM((dm, f2), BF16), pltpu.VMEM((f, dm), BF16),
                        pltpu.VMEM((EXPERT_TILE, f), BF16),
                        pltpu.SemaphoreType.DMA((2, 2))],
    )
    return pl.pallas_call(
        functools.partial(_expert_kernel, layer=layer),
        grid_spec=gs,
        out_shape=jax.ShapeDtypeStruct((p_max, dm), BF16),
        compiler_params=_cparams("arbitrary"),
        name="moe_experts",
    )(meta['tile_e'], meta['n_tiles'], meta['tile_run'], meta['run_e'], meta['n_runs'],
      xs, b1p, b_down, w_up, w_down)


def _combine_kernel(*refs, s0, nb, final_norm):
    lists = _CopyLists(refs[:6])
    x_ref, slot_ref, tw_ref, g2_ref, gfin_ref, ys_ref, o_ref, yl_ref, sem = refs[6:]
    i = pl.program_id(0)
    ni = pl.num_programs(0)
    buf = i % 2

    def fetch(step, b):
        lists.start(s0 + step, yl_ref.at[b], ys_ref, sem.at[b], False)

    @pl.when(i == 0)
    def _():
        yl_ref[...] = jnp.zeros_like(yl_ref)
        fetch(0, 0)

    @pl.when(i + 1 < ni)
    def _():
        fetch(i + 1, 1 - buf)

    lists.wait(s0 + i, ys_ref, yl_ref.at[buf], sem.at[buf])

    slot = slot_ref[...].astype(F32)
    tw = tw_ref[...]
    sidx = lax.broadcasted_iota(I32, (SUB, SLOT_ROWS), 1).astype(F32)
    pw = jnp.where(sidx == slot[:, 0:1], tw[:, 0:1], 0.0)
    for k in range(1, TOP_K):
        pw = pw + jnp.where(sidx == slot[:, k:k + 1], tw[:, k:k + 1], 0.0)
    f = jnp.dot(pw.astype(BF16), yl_ref[buf], preferred_element_type=F32)
    dm = f.shape[-1]
    x2 = x_ref[...].reshape(SUB // nb, nb, dm) + g2_ref[...][None] * f.reshape(SUB // nb, nb, dm)
    x2 = x2.reshape(SUB, dm)
    if final_norm:
        x2 = x2 * lax.rsqrt(jnp.mean(x2 * x2, axis=-1, keepdims=True) + EPS) * gfin_ref[...]
    o_ref[...] = x2


def _combine(x2d, slot_t, tw_t, g2, gfin, ys, meta, *, s0, nb, final_norm):
    n, dm = x2d.shape
    ns = n // SUB
    gs = pltpu.PrefetchScalarGridSpec(
        num_scalar_prefetch=6,
        grid=(ns,),
        in_specs=[pl.BlockSpec((SUB, dm), lambda i, *_: (i, 0)),
                  pl.BlockSpec((SUB, TOP_K), lambda i, *_: (s0 + i, 0)),
                  pl.BlockSpec((SUB, TOP_K), lambda i, *_: (s0 + i, 0)),
                  pl.BlockSpec(g2.shape, lambda i, *_: (0, 0)),
                  pl.BlockSpec(gfin.shape, lambda i, *_: (0, 0)),
                  pl.BlockSpec(memory_space=pl.ANY)],
        out_specs=pl.BlockSpec((SUB, dm), lambda i, *_: (i, 0)),
        scratch_shapes=[pltpu.VMEM((2, SLOT_ROWS, dm), BF16),
                        pltpu.SemaphoreType.DMA((2, 2))],
    )
    return pl.pallas_call(
        functools.partial(_combine_kernel, s0=s0, nb=nb, final_norm=final_norm),
        grid_spec=gs,
        out_shape=jax.ShapeDtypeStruct((n, dm), F32),
        compiler_params=_cparams("arbitrary"),
        name="moe_combine",
    )(*meta['copy_lists'], x2d, slot_t, tw_t, g2, gfin, ys)


def _routing_meta(cnt, ne):
    ns = cnt.shape[0]
    pc = (cnt + (SEG_PAD - 1)) // SEG_PAD * SEG_PAD
    loc = jnp.cumsum(pc, axis=1) - pc
    tot = jnp.sum(pc, axis=0)
    reg = (tot + (EXPERT_TILE - 1)) // EXPERT_TILE * EXPERT_TILE
    reg_end = jnp.cumsum(reg)
    base = reg_end - reg
    glob = base[None, :] + jnp.cumsum(pc, axis=0) - pc
    p_max = ns * (SUB * TOP_K + ne * (SEG_PAD - 1)) + ne * (EXPERT_TILE - SEG_PAD)
    p_max = (p_max + EXPERT_TILE - 1) // EXPERT_TILE * EXPERT_TILE
    n_tiles_max = p_max // EXPERT_TILE
    tile_start = jnp.arange(n_tiles_max, dtype=I32) * EXPERT_TILE
    tile_e = jnp.sum((tile_start[:, None] >= reg_end[None, :]).astype(I32), axis=1)
    tile_e = jnp.minimum(tile_e, ne - 1)
    eids = jnp.arange(ne, dtype=I32)
    has = reg > 0
    run_of_e = jnp.cumsum(has.astype(I32)) - 1
    run_e = jnp.sum(jnp.where(has[None, :] & (run_of_e[None, :] == eids[:, None]), eids[None, :], 0),
                    axis=1)
    tile_run = jnp.sum(jnp.where(tile_e[:, None] == eids[None, :], run_of_e[None, :], 0), axis=1)
    crow = jnp.arange(MAX_CHUNKS, dtype=I32) * SEG_PAD
    loc_end = loc + pc
    ce = jnp.sum((crow[None, :, None] >= loc_end[:, None, :]).astype(I32), axis=2)
    ce = jnp.minimum(ce, ne - 1)
    pick = ce[:, :, None] == jnp.arange(ne, dtype=I32)[None, None, :]
    cdst = crow[None, :] + jnp.sum(jnp.where(pick, (glob - loc)[:, None, :], 0), axis=2)
    nch = jnp.sum(pc, axis=1) // SEG_PAD
    seg_first = jnp.sum(jnp.where(pick, loc[:, None, :], 0), axis=2)
    seg_rows = jnp.sum(jnp.where(pick, pc[:, None, :], 0), axis=2)
    row_in_seg = crow[None, :] - seg_first
    even = (crow[None, :] < (nch * SEG_PAD)[:, None]) & (row_in_seg % (2 * SEG_PAD) == 0)
    is_pair = even & (row_in_seg + 2 * SEG_PAD <= seg_rows)
    is_single = even & (row_in_seg + 2 * SEG_PAD > seg_rows)

    def compact(mask):
        rank = jnp.cumsum(mask.astype(I32), axis=1) - 1
        sel = mask[:, None, :] & (rank[:, None, :] == jnp.arange(MAX_PAIRS, dtype=I32)[None, :, None])
        lrow = jnp.sum(jnp.where(sel, crow[None, None, :], 0), axis=2)
        grow = jnp.sum(jnp.where(sel, cdst[:, None, :], 0), axis=2)
        return [lrow.reshape(-1).astype(I32), grow.reshape(-1).astype(I32),
                jnp.sum(mask.astype(I32), axis=1)]

    return dict(copy_lists=compact(is_pair) + compact(is_single),
                nch=(jnp.sum(pc, axis=1) // SEG_PAD).astype(I32),
                tail0=(base + tot).astype(I32), tailn=((reg - tot) // SEG_PAD).astype(I32),
                locv=loc.reshape(ns, ne, 1).astype(I32),
                tile_e=tile_e, n_tiles=(reg_end[-1:] // EXPERT_TILE).astype(I32), p_max=p_max,
                tile_run=jnp.maximum(tile_run, 0).astype(I32), run_e=run_e.astype(I32),
                n_runs=jnp.sum(has.astype(I32))[None])


def kernel(x, c, ctx, c_ctx, w_ada, b_ada, g_mix, g_ffn, w_in, s5_lam_re, s5_lam_im, s5_log_dt, s5_b_re, s5_b_im, s5_c_re, s5_c_im, s5_d, s5_w_glu, s5_b_glu, lru_conv_w, lru_conv_b, lru_lam, lru_w_a, lru_b_a, lru_w_x, lru_b_x, w_proj_s5, w_proj_lru, w_out, w_router, b_router, w_up, b_up, w_down, b_down, g_final):
    nb, seq, dm = x.shape
    tc = ctx.shape[1]
    depth = w_ada.shape[0]
    r = seq // GRID_W
    s5w = s5_d.shape[-1]
    lw = lru_conv_w.shape[-1]
    ne = w_router.shape[-1]
    nl, ncx = seq * nb, tc * nb

    xl = jnp.transpose(x, (1, 0, 2)).reshape(nl, dm)
    xc = jnp.transpose(ctx, (1, 0, 2)).reshape(ncx, dm)

    pad_rows = (-(nb + 1)) % 8
    cvec = jnp.concatenate([c, c_ctx[None, :], jnp.zeros((pad_rows, dm), F32)], axis=0)
    ada = _ada_all(cvec, w_ada, b_ada)

    s5_ops = jax.vmap(_s5_operators)(s5_lam_re, s5_lam_im, s5_log_dt, s5_b_re, s5_b_im,
                                     s5_c_re, s5_c_im)
    lam = lru_lam.astype(F32)
    softplus = jnp.maximum(-lam, 0.0) + jnp.log1p(jnp.exp(-jnp.abs(lam)))
    sp_all = (-0.5 * LRU_C) * softplus[:, :, None, :]
    wg_all = jax.vmap(_lru_gate_weights)(0.5 * lru_w_a, 0.5 * lru_w_x)
    b_a_all = 0.5 * lru_b_a[:, :, None, :]
    b_x_all = 0.5 * lru_b_x[:, :, None, :]

    for l in range(depth):
        need_ctx = l < depth - 1
        mods_l =[ada[l, :nb, k * dm:(k + 1) * dm] for k in range(6)]
        mods_c = [jnp.broadcast_to(ada[l, nb:nb + 1, k * dm:(k + 1) * dm], (nb, dm)) for k in range(6)]
        gm = g_mix[l][None, :]
        gf = g_ffn[l][None, :]
        w_in_l = w_in[l].astype(BF16)

        u_l, xr_l, gg_l, m5_l, ml_l = _k1(xl, gm, mods_l[0], mods_l[1], w_in_l, nb=nb, r=r,
                                          colmajor=True, s5w=s5w, lw=lw, need_gates=True)
        outs_c = _k1(xc, gm, mods_c[0], mods_c[1], w_in_l, nb=nb, r=r, colmajor=False,
                     s5w=s5w, lw=lw, need_gates=need_ctx)
        u_c, xr_c = outs_c[0], outs_c[1]

        bm, cm, a_re, a_im = [a[l] for a in s5_ops]
        nslab = s5w // LANES
        h0 = jnp.zeros((2, nslab, nb, 2 * S5_SLAB_STATE), F32)
        ys_c, hs_c = _s5_scan(u_c.reshape(tc, nb, s5w), bm, cm, a_re, a_im, h0)
        ys_l, _ = _s5_scan(u_l.reshape(seq, nb, s5w), bm, cm, a_re, a_im, hs_c)

        sp, wg, b_a, b_x = sp_all[l], wg_all[l], b_a_all[l], b_x_all[l]
        cw = lru_conv_w[l]
        cbias = lru_conv_b[l][None, :]
        hz = jnp.zeros((2, nb, lw), F32)
        cb_c = 2 if (tc // r) % 2 == 0 else 1
        cb_l = 2 if GRID_W % 2 == 0 else 1
        yl_c, hl_c = _lru_scan(xr_c, cw, cbias, sp, wg, b_a, b_x, hz, cb=cb_c)
        yl_l, _ = _lru_scan(xr_l, cw, cbias, sp, wg, b_a, b_x, hl_c, cb=cb_l)

        k2_params = (s5_d[l][None, :], s5_w_glu[l].astype(BF16), s5_b_glu[l][None, :],
                     w_proj_s5[l].astype(BF16), w_proj_lru[l].astype(BF16), w_out[l].astype(BF16))
        wr_t = jnp.transpose(w_router[l])
        wr_hi = wr_t.astype(BF16)
        wr_lo = (wr_t - wr_hi.astype(F32)).astype(BF16)
        br = b_router[l][:, None]
        x1_l, hn_l, ti_l, tw_l, cnt_l = _k2(xl, u_l, ys_l, yl_l, gg_l, m5_l, ml_l, *k2_params,
                                            mods_l[2], gf, mods_l[3], mods_l[4], wr_hi, wr_lo, br,
                                            nb=nb, r=r, colmajor=True)
        if need_ctx:
            _, _, gg_c, m5_c, ml_c = outs_c
            x1_c, hn_c, ti_c, tw_c, cnt_c = _k2(xc, u_c, ys_c, yl_c, gg_c, m5_c, ml_c, *k2_params,
                                                mods_c[2], gf, mods_c[3], mods_c[4], wr_hi, wr_lo, br,
                                                nb=nb, r=r, colmajor=False)
            hn_parts = [hn_c, hn_l]
            ti = jnp.concatenate([ti_c, ti_l], axis=1)
            tw = jnp.concatenate([tw_c, tw_l], axis=1)
            cnt = jnp.concatenate([cnt_c, cnt_l], axis=0)
        else:
            hn_parts, ti, tw, cnt = [hn_l], ti_l, tw_l, cnt_l

        meta = _routing_meta(cnt[:, :, 0], ne)
        xs, slots = _dispatch(hn_parts, ti, meta, ne=ne)
        f2 = w_up.shape[-1]
        b1p = jnp.transpose(b_up[l].reshape(ne, f2 // (2 * LANES), LANES, 2), (0, 1, 3, 2))
        ys = _experts(xs, w_up, b1p.reshape(ne, 1, f2), w_down, b_down[:, :, None, :], meta, l)
        slot_t = jnp.transpose(slots)
        tw_t = jnp.transpose(tw)
        last = l == depth - 1
        gfin = g_final[None, :]
        if need_ctx:
            xc = _combine(x1_c, slot_t, tw_t, mods_c[5], gfin, ys, meta, s0=0, nb=nb,
                          final_norm=False)
            xl = _combine(x1_l, slot_t, tw_t, mods_l[5], gfin, ys, meta, s0=ncx // SUB,
                          nb=nb, final_norm=False)
        else:
            xl = _combine(x1_l, slot_t, tw_t, mods_l[5], gfin, ys, meta, s0=0, nb=nb,
                          final_norm=last)

    return jnp.transpose(xl.reshape(seq, nb, dm), (1, 0, 2)).astype(x.dtype)
```

```python
import functools
import math

import jax
import jax.numpy as jnp
from jax import lax
from jax.experimental import pallas as pl
from jax.experimental.pallas import tpu as pltpu

F32 = jnp.float32
BF16 = jnp.bfloat16
I32 = jnp.int32
HIGHEST = lax.Precision.HIGHEST

GRID_W = 64
S5_GROUP = 16
S5_STATE = 64
S5_DT_MIN = 0.001
S5_DT_MAX = 0.1
LRU_BLOCK = 64
LRU_C = 8.0
CONV_W = 4
TOP_K = 4
SWIGLU_LIMIT = 7.0
SWIGLU_ALPHA = 1.702
EPS = 1e-6

LANES = 128
BF16_ROWS = 16
VMEM_LIMIT = 56 * 1024 * 1024

ROW_TILE = 1024
K2_TILE = 512
S5_SLAB_GROUPS = LANES // S5_GROUP
S5_SLAB_STATE = S5_SLAB_GROUPS * S5_STATE
SUB = 256
SEG_PAD = BF16_ROWS
SLOT_ROWS = 1536
MAX_CHUNKS = SLOT_ROWS // SEG_PAD
MAX_PAIRS = MAX_CHUNKS // 2
SHORT_ROWS = SUB * TOP_K + 256
EXPERT_TILE = 1024


def _cparams(*sem, **kw):
    return pltpu.CompilerParams(dimension_semantics=sem, vmem_limit_bytes=VMEM_LIMIT, **kw)


def _gelu(x):
    c0 = math.sqrt(2.0 / math.pi)
    half = 0.5 * x
    return half + half * jnp.tanh(x * (c0 + (c0 * 0.044715) * (x * x)))


def _sigmoid(x):
    return 0.5 * jnp.tanh(0.5 * x) + 0.5


def _rms_mod(x, g, sh, sc, nb):
    tm, d = x.shape
    y = x * lax.rsqrt(jnp.mean(x * x, axis=-1, keepdims=True) + EPS) * g
    y3 = y.reshape(tm // nb, nb, d) * (1.0 + sc)[None] + sh[None]
    return y3.reshape(tm, d)


def _ada_kernel(c_ref, w_ref, b_ref, o_ref):
    c = c_ref[...]
    cond = c * _sigmoid(c)
    o_ref[0] = jnp.dot(cond, w_ref[0], preferred_element_type=F32, precision=HIGHEST) + b_ref[0]


def _ada_all(cvec, w_ada, b_ada):
    nl, d, w6 = w_ada.shape
    r = cvec.shape[0]
    tn = min(w6, 1536)
    return pl.pallas_call(
        _ada_kernel,
        grid=(nl, w6 // tn),
        in_specs=[pl.BlockSpec((r, d), lambda l, j: (0, 0)),
                  pl.BlockSpec((1, d, tn), lambda l, j: (l, 0, j)),
                  pl.BlockSpec((1, 1, tn), lambda l, j: (l, 0, j))],
        out_specs=pl.BlockSpec((1, r, tn), lambda l, j: (l, 0, j)),
        out_shape=jax.ShapeDtypeStruct((nl, r, w6), F32),
        compiler_params=_cparams("arbitrary", "arbitrary"),
        name="ada",
    )(cvec, w_ada, b_ada.reshape(nl, 1, w6))


def _k1(x2, g, sh, sc, w_in, *, nb, r, colmajor, s5w, lw, need_gates):
    n, d = x2.shape
    t = n // nb
    ncol = t // r
    if colmajor:
        tm = (t // r) * nb
        xr_spec = pl.BlockSpec((ncol, 1, nb, lw), lambda i: (0, i, 0, 0))
    else:
        cbk = max(1, min(ncol, ROW_TILE // (r * nb)))
        tm = cbk * r * nb
        xr_spec = pl.BlockSpec((cbk, r, nb, lw), lambda i: (i, 0, 0, 0))
    splits = (s5w, s5w + lw, s5w + 2 * lw, s5w + 2 * lw + d, s5w + 2 * lw + 2 * d)
    if not need_gates:
        splits = splits[:2]
    row = lambda w: pl.BlockSpec((tm, w), lambda i: (i, 0))
    full = lambda a: pl.BlockSpec(a.shape, lambda i: (0,) * a.ndim)
    out_specs = [row(s5w), xr_spec, row(lw), row(d), row(d)]
    out_shape = [jax.ShapeDtypeStruct((n, s5w), F32),
                 jax.ShapeDtypeStruct((ncol, r, nb, lw), F32),
                 jax.ShapeDtypeStruct((n, lw), BF16),
                 jax.ShapeDtypeStruct((n, d), BF16),
                 jax.ShapeDtypeStruct((n, d), BF16)]
    k = len(splits)

    def body(x_ref, g_ref, sh_ref, sc_ref, w_ref, *o_refs):
        hn = _rms_mod(x_ref[...], g_ref[...], sh_ref[...], sc_ref[...], nb).astype(BF16)
        lo = 0
        for o_ref, hi in zip(o_refs, splits):
            v = jnp.dot(hn, w_ref[:, lo:hi], preferred_element_type=F32).astype(o_ref.dtype)
            o_ref[...] = v.reshape(o_ref.shape)
            lo = hi

    w_used = w_in[:, :splits[-1]]
    return pl.pallas_call(
        body,
        grid=(n // tm,),
        in_specs=[row(d), full(g), full(sh), full(sc), full(w_used)],
        out_specs=out_specs[:k],
        out_shape=out_shape[:k],
        compiler_params=_cparams("arbitrary"),
        name="k1_in_proj",
    )(x2, g, sh, sc, w_used)


def _s5_kernel(u_ref, bm_ref, cm_ref, are_ref, aim_ref, h0_ref, y_ref, ht_ref, hs_ref, st_ref,
               *, tt, nb, nslab):
    d = pl.program_id(0)
    j = pl.program_id(1)
    nj = pl.num_programs(1)
    sw = S5_SLAB_STATE
    rows = tt * nb

    @pl.when(j == 0)
    def _():
        st_ref[...] = h0_ref[0]

    def project_in(m):
        ub = u_ref[:, :, m * LANES:(m + 1) * LANES].reshape(rows, LANES).astype(BF16)
        hs_ref[m] = jnp.dot(ub, bm_ref[0, m], preferred_element_type=F32)

    def recur(m, reverse):
        a_re = jnp.broadcast_to(are_ref[0, m], (nb, sw))
        a_im = jnp.broadcast_to(aim_ref[0, m], (nb, sw))
        hr = st_ref[m, :, :sw]
        hi = st_ref[m, :, sw:]
        for i in range(tt):
            t = tt - 1 - i if reverse else i
            rs = slice(t * nb, (t + 1) * nb)
            nr = a_re * hr - a_im * hi + hs_ref[m, rs, :sw]
            ni = a_re * hi + a_im * hr + hs_ref[m, rs, sw:]
            hs_ref[m, rs, :sw] = nr
            hs_ref[m, rs, sw:] = ni
            hr, hi = nr, ni
        st_ref[m, :, :sw] = hr
        st_ref[m, :, sw:] = hi

    def project_out(m):
        y = jnp.dot(hs_ref[m].astype(BF16), cm_ref[0, m], preferred_element_type=F32)
        y_ref[0, :, :, m * LANES:(m + 1) * LANES] = y.reshape(tt, nb, LANES)

    def run(reverse):
        project_in(0)
        for m in range(nslab):
            if m + 1 < nslab:
                project_in(m + 1)
            recur(m, reverse)
            if m > 0:
                project_out(m - 1)
        project_out(nslab - 1)

    @pl.when(d == 0)
    def _():
        run(False)

    @pl.when(d == 1)
    def _():
        run(True)

    @pl.when(j == nj - 1)
    def _():
        ht_ref[0] = st_ref[...]


def _s5_scan(u3, bm, cm, a_re, a_im, h0):
    t, nb, s5w = u3.shape
    nslab = s5w // LANES
    tt = 64 if t % 64 == 0 else t
    nj = t // tt
    sw2 = 2 * S5_SLAB_STATE
    blk = lambda d, j: j + d * (nj - 1 - 2 * j)
    return pl.pallas_call(
        functools.partial(_s5_kernel, tt=tt, nb=nb, nslab=nslab),
        grid=(2, nj),
        in_specs=[pl.BlockSpec((tt, nb, s5w), lambda d, j: (blk(d, j), 0, 0)),
                  pl.BlockSpec((1, nslab, LANES, sw2), lambda d, j: (d, 0, 0, 0)),
                  pl.BlockSpec((1, nslab, sw2, LANES), lambda d, j: (d, 0, 0, 0)),
                  pl.BlockSpec((1, nslab, 1, S5_SLAB_STATE), lambda d, j: (d, 0, 0, 0)),
                  pl.BlockSpec((1, nslab, 1, S5_SLAB_STATE), lambda d, j: (d, 0, 0, 0)),
                  pl.BlockSpec((1, nslab, nb, sw2), lambda d, j: (d, 0, 0, 0))],
        out_specs=[pl.BlockSpec((1, tt, nb, s5w), lambda d, j: (d, blk(d, j), 0, 0)),
                   pl.BlockSpec((1, nslab, nb, sw2), lambda d, j: (d, 0, 0, 0))],
        out_shape=[jax.ShapeDtypeStruct((2, t, nb, s5w), F32),
                   jax.ShapeDtypeStruct((2, nslab, nb, sw2), F32)],
        scratch_shapes=[pltpu.VMEM((nslab, tt * nb, sw2), F32),
                        pltpu.VMEM((nslab, nb, sw2), F32)],
        compiler_params=_cparams("arbitrary", "arbitrary"),
        name="s5_scan",
    )(u3, bm, cm, a_re, a_im, h0)


def _s5_operators(lam_re, lam_im, log_dt, b_re, b_im, c_re, c_im):
    lr, li = lam_re.astype(F32), lam_im.astype(F32)
    dt = jnp.exp(log_dt.astype(F32))[..., None]
    mag = jnp.exp(lr * dt)
    ab_re, ab_im = mag * jnp.cos(li * dt), mag * jnp.sin(li * dt)
    den = lr * lr + li * li
    zr = ab_re - 1.0
    q_re = (zr * lr + ab_im * li) / den
    q_im = (ab_im * lr - zr * li) / den
    br, bi = b_re.astype(F32), b_im.astype(F32)
    bb_re = q_re[..., None] * br - q_im[..., None] * bi
    bb_im = q_re[..., None] * bi + q_im[..., None] * br
    two, g, p, h = bb_re.shape
    gs = S5_SLAB_GROUPS
    nslab = g // gs
    eye = jnp.eye(gs, dtype=F32)

    def in_mat(bb):
        bb = bb.reshape(two, nslab, gs, p, h)
        m = jnp.einsum('dmgph,gk->dmghkp', bb, eye)
        return m.reshape(two, nslab, gs * h, gs * p)

    def out_mat(cc):
        cc = cc.astype(F32).reshape(two, nslab, gs, h, p)
        m = jnp.einsum('dmghp,gk->dmgpkh', cc, eye)
        return m.reshape(two, nslab, gs * p, gs * h)

    bm = jnp.concatenate([in_mat(bb_re), in_mat(bb_im)], axis=-1).astype(BF16)
    cm = jnp.concatenate([out_mat(c_re), -out_mat(c_im)], axis=-2).astype(BF16)
    a_re = ab_re.reshape(two, nslab, 1, gs * p)
    a_im = ab_im.reshape(two, nslab, 1, gs * p)
    return bm, cm, a_re, a_im


def _lru_kernel(x_ref, xp_ref, xn_ref, cw_ref, cb_ref, c1_ref, wg_ref, ba_ref, bx_ref, h0_ref,
                y_ref, ht_ref, xpad_ref, xc_ref, a_ref, b_ref, st_ref, *, cb, r, nb):
    d = pl.program_id(0)
    j = pl.program_id(1)
    nj = pl.num_programs(1)
    jeff = j + d * (nj - 1 - 2 * j)
    steps = cb * r
    rows = steps * nb
    w = x_ref.shape[-1]

    @pl.when(j == 0)
    def _():
        st_ref[...] = h0_ref[0]

    has_prev = (jeff > 0).astype(F32)
    has_next = (jeff < nj - 1).astype(F32)
    xpad_ref[0:nb, :] = xp_ref[0, 0].astype(F32) * has_prev
    xpad_ref[nb:nb + rows, :] = x_ref[...].reshape(rows, w).astype(F32)
    xpad_ref[nb + rows:3 * nb + rows, :] = xn_ref[0].reshape(2 * nb, w).astype(F32) * has_next
    acc = cb_ref[...] + cw_ref[0:1, :] * xpad_ref[0:rows, :]
    for k in range(1, CONV_W):
        acc = acc + cw_ref[k:k + 1, :] * xpad_ref[k * nb:k * nb + rows, :]
    xc_ref[...] = acc

    for s in range(w // LANES):
        ls = slice(s * LANES, (s + 1) * LANES)
        xs = xc_ref[:, ls]
        gs = jnp.dot(xs.astype(BF16), wg_ref[0, s], preferred_element_type=F32)
        tr = jnp.tanh(gs[:, :LANES] + ba_ref[0, :, ls])
        ig = 0.5 * jnp.tanh(gs[:, LANES:] + bx_ref[0, :, ls]) + 0.5
        c1 = c1_ref[0, :, ls]
        a = jnp.exp(c1 * tr + c1)
        a_ref[:, ls] = a
        s1 = 1.0 - a * a
        root = jnp.where(s1 > 0.0, s1 * lax.rsqrt(s1), 0.0)
        b_ref[:, ls] = root * (ig * xs)

    def step(i, h):
        t = jnp.where(d == 0, i, steps - 1 - i)
        r0 = pl.multiple_of(t * nb, nb)
        h = a_ref[pl.ds(r0, nb), :] * h + b_ref[pl.ds(r0, nb), :]
        y_ref[0, t // r, t % r] = h
        return h

    st_ref[...] = lax.fori_loop(0, steps, step, st_ref[...], unroll=8)

    @pl.when(j == nj - 1)
    def _():
        ht_ref[0] = st_ref[...]


def _lru_scan(x4, conv_w, conv_b, sp, wg, b_a, b_x, h0, *, cb):
    c, r, nb, w = x4.shape
    nj = c // cb
    rows = cb * r * nb
    blk = lambda d, j: j + d * (nj - 1 - 2 * j)
    full2 = lambda a: pl.BlockSpec(a.shape, lambda d, j: (0,) * a.ndim)
    perdir = lambda a: pl.BlockSpec((1,) + a.shape[1:], lambda d, j: (d,) + (0,) * (a.ndim - 1))
    return pl.pallas_call(
        functools.partial(_lru_kernel, cb=cb, r=r, nb=nb),
        grid=(2, nj),
        in_specs=[pl.BlockSpec((cb, r, nb, w), lambda d, j: (blk(d, j), 0, 0, 0)),
                  pl.BlockSpec((1, 1, nb, w),
                               lambda d, j: (jnp.maximum(blk(d, j) * cb - 1, 0), r - 1, 0, 0)),
                  pl.BlockSpec((1, 2, nb, w),
                               lambda d, j: (jnp.minimum((blk(d, j) + 1) * cb, c - 1), 0, 0, 0)),
                  full2(conv_w), full2(conv_b), perdir(sp), perdir(wg), perdir(b_a), perdir(b_x),
                  perdir(h0)],
        out_specs=[pl.BlockSpec((1, cb, r, nb, w), lambda d, j: (d, blk(d, j), 0, 0, 0)),
                   pl.BlockSpec((1, nb, w), lambda d, j: (d, 0, 0))],
        out_shape=[jax.ShapeDtypeStruct((2, c, r, nb, w), F32),
                   jax.ShapeDtypeStruct((2, nb, w), F32)],
        scratch_shapes=[pltpu.VMEM((rows + 3 * nb, w), F32),
                        pltpu.VMEM((rows, w), F32),
                        pltpu.VMEM((rows, w), F32),
                        pltpu.VMEM((rows, w), F32),
                        pltpu.VMEM((nb, w), F32)],
        compiler_params=_cparams("arbitrary", "arbitrary"),
        name="lru_scan",
    )(x4, x4, x4, conv_w, conv_b, sp, wg, b_a, b_x, h0)


def _lru_gate_weights(w_a, w_x):
    two, nblk, k, _ = w_a.shape
    per = LANES // k
    eye = jnp.eye(per, dtype=F32)

    def slabs(wm):
        wm = wm.astype(F32).reshape(two, nblk // per, per, k, k)
        m = jnp.einsum('dsgij,gk->dsgikj', wm, eye)
        return m.reshape(two, nblk // per, LANES, LANES)

    return jnp.concatenate([slabs(w_a), slabs(w_x)], axis=-1).astype(BF16)


def _k2_kernel(x_ref, u_ref, ys_ref, yl_ref, gg_ref, m5_ref, ml_ref, d5_ref, wglu_ref, bglu_ref,
               wp5_ref, wpl_ref, wo_ref, g1_ref, gf_ref, sh_ref, sc_ref, wrh_ref, wrl_ref, br_ref,
               x1_ref, hn_ref, ti_ref, tw_ref, cnt_ref, *, nb):
    tm, dm = x_ref.shape
    u = u_ref[...]
    y5 = ys_ref[0] + ys_ref[1] + d5_ref[...] * u
    gy = _gelu(y5)
    o5 = gy * _sigmoid(jnp.dot(gy.astype(BF16), wglu_ref[...], preferred_element_type=F32)
                       + bglu_ref[...])
    yl = (yl_ref[0] + yl_ref[1]).reshape(tm, -1)
    ol = _gelu(gg_ref[...].astype(F32)) * yl
    merged = (_sigmoid(m5_ref[...]).astype(F32)
              * jnp.dot(o5.astype(BF16), wp5_ref[...], preferred_element_type=F32)
              + _sigmoid(ml_ref[...]).astype(F32)
              * jnp.dot(ol.astype(BF16), wpl_ref[...], preferred_element_type=F32))
    mix = jnp.dot(merged.astype(BF16), wo_ref[...], preferred_element_type=F32)
    x1 = (x_ref[...].reshape(tm // nb, nb, dm) + g1_ref[...][None] * mix.reshape(tm // nb, nb, dm))
    x1 = x1.reshape(tm, dm)
    x1_ref[...] = x1
    hn = _rms_mod(x1, gf_ref[...], sh_ref[...], sc_ref[...], nb)
    hn_hi = hn.astype(BF16)
    hn_ref[...] = hn_hi
    hn_lo = (hn - hn_hi.astype(F32)).astype(BF16)

    nt = (((1,), (1,)), ((), ()))
    logits = (lax.dot_general(wrh_ref[...], hn_hi, nt, preferred_element_type=F32)
              + lax.dot_general(wrl_ref[...], hn_hi, nt, preferred_element_type=F32)
              + lax.dot_general(wrh_ref[...], hn_lo, nt, preferred_element_type=F32)) + br_ref[...]
    ne = logits.shape[0]
    eidx = lax.broadcasted_iota(I32, (ne, tm), 0).astype(F32)
    vals = logits
    tv, ti = [], []
    for _ in range(TOP_K):
        m = jnp.max(vals, axis=0, keepdims=True)
        idx = jnp.min(jnp.where(vals == m, eidx, float(ne)), axis=0, keepdims=True)
        tv.append(m)
        ti.append(idx)
        vals = jnp.where(eidx == idx, -jnp.inf, vals)
    ex = [jnp.exp(v - tv[0]) for v in tv]
    den = ex[0] + ex[1] + ex[2] + ex[3]
    ti_ref[...] = jnp.concatenate(ti, axis=0).astype(I32)
    tw_ref[...] = jnp.concatenate([e / den for e in ex], axis=0)
    sel = (eidx == ti[0]) | (eidx == ti[1]) | (eidx == ti[2]) | (eidx == ti[3])
    self32 = jnp.where(sel, 1.0, 0.0)
    for sb in range(tm // SUB):
        cnt_ref[sb] = jnp.sum(self32[:, sb * SUB:(sb + 1) * SUB], axis=1, keepdims=True).astype(I32)


def _k2(x2, u, ys, yl5, gg, m5, ml, d5, wglu, bglu, wp5, wpl, wo, g1, gf, sh, sc, wr_hi, wr_lo, br,
        *, nb, r, colmajor):
    n, dm = x2.shape
    t = n // nb
    ncol = t // r
    lw = yl5.shape[-1]
    s5w = u.shape[-1]
    ne = wr_hi.shape[0]
    if colmajor:
        cbk = max(1, min(ncol, K2_TILE // nb))
        per_row = ncol // cbk
        tm = cbk * nb
        yl_spec = pl.BlockSpec((2, cbk, 1, nb, lw),
                               lambda i: (0, i % per_row, i // per_row, 0, 0))
    else:
        cbk = max(1, min(ncol, K2_TILE // (r * nb)))
        tm = cbk * r * nb
        yl_spec = pl.BlockSpec((2, cbk, r, nb, lw), lambda i: (0, i, 0, 0, 0))
    row = lambda w: pl.BlockSpec((tm, w), lambda i: (i, 0))
    full = lambda a: pl.BlockSpec(a.shape, lambda i: (0,) * a.ndim)
    ys2 = ys.reshape(2, n, s5w)
    params = (d5, wglu, bglu, wp5, wpl, wo, g1, gf, sh, sc, wr_hi, wr_lo, br)
    return pl.pallas_call(
        functools.partial(_k2_kernel, nb=nb),
        grid=(n // tm,),
        in_specs=[row(dm), row(s5w), pl.BlockSpec((2, tm, s5w), lambda i: (0, i, 0)), yl_spec,
                  row(lw), row(dm), row(dm)] + [full(p) for p in params],
        out_specs=[row(dm), row(dm),
                   pl.BlockSpec((TOP_K, tm), lambda i: (0, i)),
                   pl.BlockSpec((TOP_K, tm), lambda i: (0, i)),
                   pl.BlockSpec((tm // SUB, ne, 1), lambda i: (i, 0, 0))],
        out_shape=[jax.ShapeDtypeStruct((n, dm), F32),
                   jax.ShapeDtypeStruct((n, dm), BF16),
                   jax.ShapeDtypeStruct((TOP_K, n), I32),
                   jax.ShapeDtypeStruct((TOP_K, n), F32),
                   jax.ShapeDtypeStruct((n // SUB, ne, 1), I32)],
        compiler_params=_cparams("arbitrary"),
        name="k2_mix_out",
    )(x2, u, ys2, yl5, gg, m5, ml, *params)


def _start_copies(n, rows, lrow_ref, grow_ref, base, local_ref, sorted_ref, sem, to_sorted,
                  priority):
    def one(k, _):
        lo = pl.multiple_of(lrow_ref[base + k], SEG_PAD)
        g = pl.multiple_of(grow_ref[base + k], SEG_PAD)
        loc = local_ref.at[pl.ds(lo, rows)]
        srt = sorted_ref.at[pl.ds(g, rows)]
        if to_sorted:
            pltpu.make_async_copy(loc, srt, sem).start(priority)
        else:
            pltpu.make_async_copy(srt, loc, sem).start(priority)
        return 0

    lax.fori_loop(0, n, one, 0)


def _wait_copies(n, rows, src_ref, dst_ref, sem):
    def one(k, _):
        pltpu.make_async_copy(src_ref.at[pl.ds(0, rows)], dst_ref.at[pl.ds(0, rows)], sem).wait()
        return 0

    lax.fori_loop(0, n, one, 0)


class _CopyLists:
    def __init__(self, refs):
        self.pl_ref, self.pg_ref, self.pn_ref, self.sl_ref, self.sg_ref, self.sn_ref = refs

    def start(self, s, local_ref, sorted_ref, sems, to_sorted):
        _start_copies(self.pn_ref[s], 2 * SEG_PAD, self.pl_ref, self.pg_ref, s * MAX_PAIRS,
                      local_ref, sorted_ref, sems.at[0], to_sorted, 0)
        _start_copies(self.sn_ref[s], SEG_PAD, self.sl_ref, self.sg_ref, s * MAX_PAIRS,
                      local_ref, sorted_ref, sems.at[1], to_sorted, 1)

    def wait(self, s, src_ref, dst_ref, sems):
        _wait_copies(self.pn_ref[s], 2 * SEG_PAD, src_ref, dst_ref, sems.at[0])
        _wait_copies(self.sn_ref[s], SEG_PAD, src_ref, dst_ref, sems.at[1])


def _dispatch_kernel(*refs, ne, part_starts):
    lists = _CopyLists(refs[:6])
    nch_ref, tail0_ref, tailn_ref = refs[6:9]
    nparts = len(part_starts)
    hn_refs = refs[9:9 + nparts]
    ti_ref, locv_ref, xs_ref, slot_ref, xg_ref, zero_ref, sem, tsem = refs[9 + nparts:]
    s = pl.program_id(0)
    ns = pl.num_programs(0)
    buf = s % 2

    @pl.when(s == 0)
    def _():
        zero_ref[...] = jnp.zeros_like(zero_ref)

        def per_e(e, _):
            def per_chunk(c, _):
                d0 = pl.multiple_of(tail0_ref[e] + c * SEG_PAD, SEG_PAD)
                pltpu.make_async_copy(zero_ref, xs_ref.at[pl.ds(d0, SEG_PAD)], tsem).start()
                return 0
            lax.fori_loop(0, tailn_ref[e], per_chunk, 0)
            return 0
        lax.fori_loop(0, ne, per_e, 0)

        def per_e_wait(e, _):
            def per_chunk(c, _):
                pltpu.make_async_copy(zero_ref, xs_ref.at[pl.ds(0, SEG_PAD)], tsem).wait()
                return 0
            lax.fori_loop(0, tailn_ref[e], per_chunk, 0)
            return 0
        lax.fori_loop(0, ne, per_e_wait, 0)

    ti = ti_ref[...].astype(F32)
    eidx = lax.broadcasted_iota(I32, (ne, SUB), 0).astype(F32)
    hit = [eidx == ti[k:k + 1] for k in range(TOP_K)]
    sel = hit[0] | hit[1] | hit[2] | hit[3]
    upper = (lax.broadcasted_iota(I32, (SUB, SUB), 0)
             < lax.broadcasted_iota(I32, (SUB, SUB), 1))
    rank = jnp.dot(jnp.where(sel, 1.0, 0.0).astype(BF16), jnp.where(upper, 1.0, 0.0).astype(BF16),
                   preferred_element_type=F32)
    slot_e = locv_ref[0].astype(F32) + rank
    slots = [jnp.sum(jnp.where(hit[k], slot_e, 0.0), axis=0, keepdims=True) for k in range(TOP_K)]
    slot_ref[...] = jnp.concatenate(slots, axis=0).astype(I32)

    hn = hn_refs[0][...]
    for p in range(1, nparts):
        hn = jnp.where(s >= part_starts[p], hn_refs[p][...], hn)
    blk = 256
    used_rows = nch_ref[s] * SEG_PAD

    def gather_rows(rb):
        sidx = (lax.broadcasted_iota(I32, (blk, SUB), 0) + rb * blk).astype(F32)
        p = (sidx == slots[0]) | (sidx == slots[1]) | (sidx == slots[2]) | (sidx == slots[3])
        xg = jnp.dot(jnp.where(p, 1.0, 0.0).astype(BF16), hn, preferred_element_type=F32)
        xg_ref[buf, rb * blk:(rb + 1) * blk, :] = xg.astype(BF16)

    for rb in range(SLOT_ROWS // blk):
        if (rb + 1) * blk <= SHORT_ROWS:
            gather_rows(rb)
        else:
            pl.when(used_rows > rb * blk)(functools.partial(gather_rows, rb))

    lists.start(s, xg_ref.at[buf], xs_ref, sem.at[buf], True)

    @pl.when(s > 0)
    def _():
        lists.wait(s - 1, xg_ref.at[1 - buf], xs_ref, sem.at[1 - buf])

    @pl.when(s == ns - 1)
    def _():
        lists.wait(s, xg_ref.at[buf], xs_ref, sem.at[buf])


def _dispatch(hn_parts, ti, meta, *, ne):
    dm = hn_parts[0].shape[1]
    counts = [h.shape[0] // SUB for h in hn_parts]
    starts = [sum(counts[:p]) for p in range(len(counts))]
    ns = sum(counts)
    n = ns * SUB

    def part_spec(start, count):
        return pl.BlockSpec((SUB, dm), lambda s, *_: (jnp.clip(s - start, 0, count - 1), 0))

    assert ne <= MAX_PAIRS
    gs = pltpu.PrefetchScalarGridSpec(
        num_scalar_prefetch=9,
        grid=(ns,),
        in_specs=[part_spec(st, ct) for st, ct in zip(starts, counts)]
                 + [pl.BlockSpec((TOP_K, SUB), lambda s, *_: (0, s)),
                    pl.BlockSpec((1, ne, 1), lambda s, *_: (s, 0, 0))],
        out_specs=[pl.BlockSpec(memory_space=pl.ANY),
                   pl.BlockSpec((TOP_K, SUB), lambda s, *_: (0, s))],
        scratch_shapes=[pltpu.VMEM((2, SLOT_ROWS, dm), BF16),
                        pltpu.VMEM((SEG_PAD, dm), BF16),
                        pltpu.SemaphoreType.DMA((2, 2)),
                        pltpu.SemaphoreType.DMA(())],
    )
    return pl.pallas_call(
        functools.partial(_dispatch_kernel, ne=ne, part_starts=tuple(starts)),
        grid_spec=gs,
        out_shape=[jax.ShapeDtypeStruct((meta['p_max'], dm), BF16),
                   jax.ShapeDtypeStruct((TOP_K, n), I32)],
        compiler_params=_cparams("arbitrary"),
        name="moe_dispatch",
    )(*meta['copy_lists'], meta['nch'], meta['tail0'], meta['tailn'], *hn_parts, ti, meta['locv'])


def _expert_kernel(te_ref, nt_ref, run_ref, rexp_ref, nrun_ref, x_ref, b1_ref, bd_ref,
                   wu_hbm, wd_hbm, y_ref, wu_buf, wd_buf, w1_ref, w2_ref, act_ref, sem, *, layer):
    i = pl.program_id(0)
    live = i < nt_ref[0]
    run = run_ref[i]
    new_run = (i == 0) | (run_ref[jnp.maximum(i - 1, 0)] != run)
    pair = 2 * LANES
    nblk = wu_buf.shape[-1] // pair

    def weight_copies(expert, slot):
        return (pltpu.make_async_copy(wu_hbm.at[layer, expert], wu_buf.at[slot], sem.at[0, slot]),
                pltpu.make_async_copy(wd_hbm.at[layer, expert], wd_buf.at[slot], sem.at[1, slot]))

    @pl.when(live & new_run)
    def _():
        slot = run % 2

        @pl.when(run == 0)
        def _():
            for cp in weight_copies(rexp_ref[0], 0):
                cp.start()

        for cp in weight_copies(rexp_ref[run], slot):
            cp.wait()

        @pl.when(run + 1 < nrun_ref[0])
        def _():
            for cp in weight_copies(rexp_ref[run + 1], 1 - slot):
                cp.start(1)

        rr = lax.broadcasted_iota(I32, (pair, pair), 0)
        cc = lax.broadcasted_iota(I32, (pair, pair), 1)
        src = jnp.where(cc < LANES, 2 * cc, 2 * (cc - LANES) + 1)
        perm = jnp.where(rr == src, 1.0, 0.0).astype(BF16)
        for jb in range(nblk):
            cols = slice(jb * pair, (jb + 1) * pair)
            w1_ref[:, cols] = jnp.dot(wu_buf[slot, :, cols].astype(BF16), perm,
                                      preferred_element_type=F32).astype(BF16)
        w2_ref[...] = wd_buf[slot].astype(BF16)

    @pl.when(live)
    def _():
        x = x_ref[...]
        for jb in range(nblk):
            cols = slice(jb * pair, (jb + 1) * pair)
            hb = jnp.dot(x, w1_ref[:, cols], preferred_element_type=F32) + b1_ref[0, :, cols]
            gate = jnp.minimum(hb[:, :LANES], SWIGLU_LIMIT)
            up = jnp.clip(hb[:, LANES:], -SWIGLU_LIMIT, SWIGLU_LIMIT)
            act = gate * _sigmoid(SWIGLU_ALPHA * gate) * (up + 1.0)
            act_ref[:, jb * LANES:(jb + 1) * LANES] = act.astype(BF16)
        y = jnp.dot(act_ref[...], w2_ref[...], preferred_element_type=F32) + bd_ref[0, 0]
        y_ref[...] = y.astype(y_ref.dtype)


def _experts(xs, w_up, b1p, w_down, b_down, meta, layer):
    p_max, dm = xs.shape
    _, ne, _, f2 = w_up.shape
    f = f2 // 2
    n_tiles = p_max // EXPERT_TILE
    tile = lambda i, te, nt, *_: (jnp.minimum(i, jnp.maximum(nt[0] - 1, 0)), 0)
    byexp = lambda i, te, *_: (te[i], 0, 0)
    bylayer = lambda i, te, *_: (layer, te[i], 0, 0)
    gs = pltpu.PrefetchScalarGridSpec(
        num_scalar_prefetch=5,
        grid=(n_tiles,),
        in_specs=[pl.BlockSpec((EXPERT_TILE, dm), tile),
                  pl.BlockSpec((1, 1, f2), byexp), pl.BlockSpec((1, 1, 1, dm), bylayer),
                  pl.BlockSpec(memory_space=pl.ANY), pl.BlockSpec(memory_space=pl.ANY)],
        out_specs=pl.BlockSpec((EXPERT_TILE, dm), tile),
        scratch_shapes=[pltpu.VMEM((2, dm, f2), F32), pltpu.VMEM((2, f, dm), F32),
                        pltpu.VMEM((dm, f2), BF16), pltpu.VMEM((f, dm), BF16),
                        pltpu.VMEM((EXPERT_TILE, f), BF16),
                        pltpu.SemaphoreType.DMA((2, 2))],
    )
    return pl.pallas_call(
        functools.partial(_expert_kernel, layer=layer),
        grid_spec=gs,
        out_shape=jax.ShapeDtypeStruct((p_max, dm), BF16),
        compiler_params=_cparams("arbitrary"),
        name="moe_experts",
    )(meta['tile_e'], meta['n_tiles'], meta['tile_run'], meta['run_e'], meta['n_runs'],
      xs, b1p, b_down, w_up, w_down)


def _combine_kernel(*refs, s0, nb, final_norm):
    lists = _CopyLists(refs[:6])
    x_ref, slot_ref, tw_ref, g2_ref, gfin_ref, ys_ref, o_ref, yl_ref, sem = refs[6:]
    i = pl.program_id(0)
    ni = pl.num_programs(0)
    buf = i % 2

    def fetch(step, b):
        lists.start(s0 + step, yl_ref.at[b], ys_ref, sem.at[b], False)

    @pl.when(i == 0)
    def _():
        yl_ref[...] = jnp.zeros_like(yl_ref)
        fetch(0, 0)

    @pl.when(i + 1 < ni)
    def _():
        fetch(i + 1, 1 - buf)

    lists.wait(s0 + i, ys_ref, yl_ref.at[buf], sem.at[buf])

    slot = slot_ref[...].astype(F32)
    tw = tw_ref[...]
    sidx = lax.broadcasted_iota(I32, (SUB, SLOT_ROWS), 1).astype(F32)
    pw = jnp.where(sidx == slot[:, 0:1], tw[:, 0:1], 0.0)
    for k in range(1, TOP_K):
        pw = pw + jnp.where(sidx == slot[:, k:k + 1], tw[:, k:k + 1], 0.0)
    f = jnp.dot(pw.astype(BF16), yl_ref[buf], preferred_element_type=F32)
    dm = f.shape[-1]
    x2 = x_ref[...].reshape(SUB // nb, nb, dm) + g2_ref[...][None] * f.reshape(SUB // nb, nb, dm)
    x2 = x2.reshape(SUB, dm)
    if final_norm:
        x2 = x2 * lax.rsqrt(jnp.mean(x2 * x2, axis=-1, keepdims=True) + EPS) * gfin_ref[...]
    o_ref[...] = x2


def _combine(x2d, slot_t, tw_t, g2, gfin, ys, meta, *, s0, nb, final_norm):
    n, dm = x2d.shape
    ns = n // SUB
    gs = pltpu.PrefetchScalarGridSpec(
        num_scalar_prefetch=6,
        grid=(ns,),
        in_specs=[pl.BlockSpec((SUB, dm), lambda i, *_: (i, 0)),
                  pl.BlockSpec((SUB, TOP_K), lambda i, *_: (s0 + i, 0)),
                  pl.BlockSpec((SUB, TOP_K), lambda i, *_: (s0 + i, 0)),
                  pl.BlockSpec(g2.shape, lambda i, *_: (0, 0)),
                  pl.BlockSpec(gfin.shape, lambda i, *_: (0, 0)),
                  pl.BlockSpec(memory_space=pl.ANY)],
        out_specs=pl.BlockSpec((SUB, dm), lambda i, *_: (i, 0)),
        scratch_shapes=[pltpu.VMEM((2, SLOT_ROWS, dm), BF16),
                        pltpu.SemaphoreType.DMA((2, 2))],
    )
    return pl.pallas_call(
        functools.partial(_combine_kernel, s0=s0, nb=nb, final_norm=final_norm),
        grid_spec=gs,
        out_shape=jax.ShapeDtypeStruct((n, dm), F32),
        compiler_params=_cparams("arbitrary"),
        name="moe_combine",
    )(*meta['copy_lists'], x2d, slot_t, tw_t, g2, gfin, ys)


def _routing_meta(cnt, ne):
    ns = cnt.shape[0]
    pc = (cnt + (SEG_PAD - 1)) // SEG_PAD * SEG_PAD
    loc = jnp.cumsum(pc, axis=1) - pc
    tot = jnp.sum(pc, axis=0)
    reg = (tot + (EXPERT_TILE - 1)) // EXPERT_TILE * EXPERT_TILE
    reg_end = jnp.cumsum(reg)
    base = reg_end - reg
    glob = base[None, :] + jnp.cumsum(pc, axis=0) - pc
    p_max = ns * (SUB * TOP_K + ne * (SEG_PAD - 1)) + ne * (EXPERT_TILE - SEG_PAD)
    p_max = (p_max + EXPERT_TILE - 1) // EXPERT_TILE * EXPERT_TILE
    n_tiles_max = p_max // EXPERT_TILE
    tile_start = jnp.arange(n_tiles_max, dtype=I32) * EXPERT_TILE
    tile_e = jnp.sum((tile_start[:, None] >= reg_end[None, :]).astype(I32), axis=1)
    tile_e = jnp.minimum(tile_e, ne - 1)
    eids = jnp.arange(ne, dtype=I32)
    has = reg > 0
    run_of_e = jnp.cumsum(has.astype(I32)) - 1
    run_e = jnp.sum(jnp.where(has[None, :] & (run_of_e[None, :] == eids[:, None]), eids[None, :], 0),
                    axis=1)
    tile_run = jnp.sum(jnp.where(tile_e[:, None] == eids[None, :], run_of_e[None, :], 0), axis=1)
    crow = jnp.arange(MAX_CHUNKS, dtype=I32) * SEG_PAD
    loc_end = loc + pc
    ce = jnp.sum((crow[None, :, None] >= loc_end[:, None, :]).astype(I32), axis=2)
    ce = jnp.minimum(ce, ne - 1)
    pick = ce[:, :, None] == jnp.arange(ne, dtype=I32)[None, None, :]
    cdst = crow[None, :] + jnp.sum(jnp.where(pick, (glob - loc)[:, None, :], 0), axis=2)
    nch = jnp.sum(pc, axis=1) // SEG_PAD
    seg_first = jnp.sum(jnp.where(pick, loc[:, None, :], 0), axis=2)
    seg_rows = jnp.sum(jnp.where(pick, pc[:, None, :], 0), axis=2)
    row_in_seg = crow[None, :] - seg_first
    even = (crow[None, :] < (nch * SEG_PAD)[:, None]) & (row_in_seg % (2 * SEG_PAD) == 0)
    is_pair = even & (row_in_seg + 2 * SEG_PAD <= seg_rows)
    is_single = even & (row_in_seg + 2 * SEG_PAD > seg_rows)

    def compact(mask):
        rank = jnp.cumsum(mask.astype(I32), axis=1) - 1
        sel = mask[:, None, :] & (rank[:, None, :] == jnp.arange(MAX_PAIRS, dtype=I32)[None, :, None])
        lrow = jnp.sum(jnp.where(sel, crow[None, None, :], 0), axis=2)
        grow = jnp.sum(jnp.where(sel, cdst[:, None, :], 0), axis=2)
        return [lrow.reshape(-1).astype(I32), grow.reshape(-1).astype(I32),
                jnp.sum(mask.astype(I32), axis=1)]

    return dict(copy_lists=compact(is_pair) + compact(is_single),
                nch=(jnp.sum(pc, axis=1) // SEG_PAD).astype(I32),
                tail0=(base + tot).astype(I32), tailn=((reg - tot) // SEG_PAD).astype(I32),
                locv=loc.reshape(ns, ne, 1).astype(I32),
                tile_e=tile_e, n_tiles=(reg_end[-1:] // EXPERT_TILE).astype(I32), p_max=p_max,
                tile_run=jnp.maximum(tile_run, 0).astype(I32), run_e=run_e.astype(I32),
                n_runs=jnp.sum(has.astype(I32))[None])


def kernel(x, c, ctx, c_ctx, w_ada, b_ada, g_mix, g_ffn, w_in, s5_lam_re, s5_lam_im, s5_log_dt, s5_b_re, s5_b_im, s5_c_re, s5_c_im, s5_d, s5_w_glu, s5_b_glu, lru_conv_w, lru_conv_b, lru_lam, lru_w_a, lru_b_a, lru_w_x, lru_b_x, w_proj_s5, w_proj_lru, w_out, w_router, b_router, w_up, b_up, w_down, b_down, g_final):
    nb, seq, dm = x.shape
    tc = ctx.shape[1]
    depth = w_ada.shape[0]
    r = seq // GRID_W
    s5w = s5_d.shape[-1]
    lw = lru_conv_w.shape[-1]
    ne = w_router.shape[-1]
    nl, ncx = seq * nb, tc * nb

    xl = jnp.transpose(x, (1, 0, 2)).reshape(nl, dm)
    xc = jnp.transpose(ctx, (1, 0, 2)).reshape(ncx, dm)

    pad_rows = (-(nb + 1)) % 8
    cvec = jnp.concatenate([c, c_ctx[None, :], jnp.zeros((pad_rows, dm), F32)], axis=0)
    ada = _ada_all(cvec, w_ada, b_ada)

    s5_ops = jax.vmap(_s5_operators)(s5_lam_re, s5_lam_im, s5_log_dt, s5_b_re, s5_b_im,
                                     s5_c_re, s5_c_im)
    lam = lru_lam.astype(F32)
    softplus = jnp.maximum(-lam, 0.0) + jnp.log1p(jnp.exp(-jnp.abs(lam)))
    sp_all = (-0.5 * LRU_C) * softplus[:, :, None, :]
    wg_all = jax.vmap(_lru_gate_weights)(0.5 * lru_w_a, 0.5 * lru_w_x)
    b_a_all = 0.5 * lru_b_a[:, :, None, :]
    b_x_all = 0.5 * lru_b_x[:, :, None, :]

    for l in range(depth):
        need_ctx = l < depth - 1
        mods_l =[ada[l, :nb, k * dm:(k + 1) * dm] for k in range(6)]
        mods_c = [jnp.broadcast_to(ada[l, nb:nb + 1, k * dm:(k + 1) * dm], (nb, dm)) for k in range(6)]
        gm = g_mix[l][None, :]
        gf = g_ffn[l][None, :]
        w_in_l = w_in[l].astype(BF16)

        u_l, xr_l, gg_l, m5_l, ml_l = _k1(xl, gm, mods_l[0], mods_l[1], w_in_l, nb=nb, r=r,
                                          colmajor=True, s5w=s5w, lw=lw, need_gates=True)
        outs_c = _k1(xc, gm, mods_c[0], mods_c[1], w_in_l, nb=nb, r=r, colmajor=False,
                     s5w=s5w, lw=lw, need_gates=need_ctx)
        u_c, xr_c = outs_c[0], outs_c[1]

        bm, cm, a_re, a_im = [a[l] for a in s5_ops]
        nslab = s5w // LANES
        h0 = jnp.zeros((2, nslab, nb, 2 * S5_SLAB_STATE), F32)
        ys_c, hs_c = _s5_scan(u_c.reshape(tc, nb, s5w), bm, cm, a_re, a_im, h0)
        ys_l, _ = _s5_scan(u_l.reshape(seq, nb, s5w), bm, cm, a_re, a_im, hs_c)

        sp, wg, b_a, b_x = sp_all[l], wg_all[l], b_a_all[l], b_x_all[l]
        cw = lru_conv_w[l]
        cbias = lru_conv_b[l][None, :]
        hz = jnp.zeros((2, nb, lw), F32)
        cb_c = 2 if (tc // r) % 2 == 0 else 1
        cb_l = 2 if GRID_W % 2 == 0 else 1
        yl_c, hl_c = _lru_scan(xr_c, cw, cbias, sp, wg, b_a, b_x, hz, cb=cb_c)
        yl_l, _ = _lru_scan(xr_l, cw, cbias, sp, wg, b_a, b_x, hl_c, cb=cb_l)

        k2_params = (s5_d[l][None, :], s5_w_glu[l].astype(BF16), s5_b_glu[l][None, :],
                     w_proj_s5[l].astype(BF16), w_proj_lru[l].astype(BF16), w_out[l].astype(BF16))
        wr_t = jnp.transpose(w_router[l])
        wr_hi = wr_t.astype(BF16)
        wr_lo = (wr_t - wr_hi.astype(F32)).astype(BF16)
        br = b_router[l][:, None]
        x1_l, hn_l, ti_l, tw_l, cnt_l = _k2(xl, u_l, ys_l, yl_l, gg_l, m5_l, ml_l, *k2_params,
                                            mods_l[2], gf, mods_l[3], mods_l[4], wr_hi, wr_lo, br,
                                            nb=nb, r=r, colmajor=True)
        if need_ctx:
            _, _, gg_c, m5_c, ml_c = outs_c
            x1_c, hn_c, ti_c, tw_c, cnt_c = _k2(xc, u_c, ys_c, yl_c, gg_c, m5_c, ml_c, *k2_params,
                                                mods_c[2], gf, mods_c[3], mods_c[4], wr_hi, wr_lo, br,
                                                nb=nb, r=r, colmajor=False)
            hn_parts = [hn_c, hn_l]
            ti = jnp.concatenate([ti_c, ti_l], axis=1)
            tw = jnp.concatenate([tw_c, tw_l], axis=1)
            cnt = jnp.concatenate([cnt_c, cnt_l], axis=0)
        else:
            hn_parts, ti, tw, cnt = [hn_l], ti_l, tw_l, cnt_l

        meta = _routing_meta(cnt[:, :, 0], ne)
        xs, slots = _dispatch(hn_parts, ti, meta, ne=ne)
        f2 = w_up.shape[-1]
        b1p = jnp.transpose(b_up[l].reshape(ne, f2 // (2 * LANES), LANES, 2), (0, 1, 3, 2))
        ys = _experts(xs, w_up, b1p.reshape(ne, 1, f2), w_down, b_down[:, :, None, :], meta, l)
        slot_t = jnp.transpose(slots)
        tw_t = jnp.transpose(tw)
        last = l == depth - 1
        gfin = g_final[None, :]
        if need_ctx:
            xc = _combine(x1_c, slot_t, tw_t, mods_c[5], gfin, ys, meta, s0=0, nb=nb,
                          final_norm=False)
            xl = _combine(x1_l, slot_t, tw_t, mods_l[5], gfin, ys, meta, s0=ncx // SUB,
                          nb=nb, final_norm=False)
        else:
            xl = _combine(x1_l, slot_t, tw_t, mods_l[5], gfin, ys, meta, s0=0, nb=nb,
                          final_norm=last)

    return jnp.transpose(xl.reshape(seq, nb, dm), (1, 0, 2)).astype(x.dtype)
```
